```python
import math
import jax, jax.numpy as jnp
from jax import lax
import numpy as np

D_MODEL = 1024
BATCH = 2
SEQ = 16384
DEPTH = 1

MEM_LEN = 256
EPS = 1e-6

SSD_HEADS = 16
SSD_HEAD_DIM = 64
SSD_DIM = SSD_HEADS * SSD_HEAD_DIM
SSD_GROUPS = 2
SSD_HEADS_PER_GROUP = SSD_HEADS // SSD_GROUPS
SSD_STATE = 128
SSD_CONV = 4
SSD_CHUNK = 128
SSD_CONV_CH = SSD_DIM + 2 * SSD_GROUPS * SSD_STATE

HG_HEADS = 8
HG_K = 128
HG_V = 128
HG_KDIM = HG_HEADS * HG_K
HG_VDIM = HG_HEADS * HG_V
HG_CHUNK = 64

D_MIX = SSD_DIM + HG_VDIM
N_IN = SSD_DIM + SSD_CONV_CH + SSD_HEADS + 2 * HG_KDIM + 2 * HG_VDIM

XA_HEADS = 4
XA_HEAD_DIM = D_MODEL // XA_HEADS

FFN_DIM = -(-8 * D_MODEL // (3 * 256)) * 256

kernel_name = "hybrid_ssd_hgrn2_xattn_block"


def rmsnorm(x, w):
    xf = x.astype(jnp.float32)
    y = xf * lax.rsqrt(jnp.mean(xf * xf, axis=-1, keepdims=True) + EPS)
    return (y * w.astype(jnp.float32)).astype(x.dtype)


def causal_depthwise_conv(u, w, b):
    ch = u.shape[-1]
    out = lax.conv_general_dilated(
        u, w[:, None, :].astype(u.dtype), window_strides=(1,),
        padding=[(w.shape[0] - 1, 0)], dimension_numbers=("NWC", "WIO", "NWC"),
        feature_group_count=ch)
    return out + b.astype(u.dtype)


def ssd_mixer(xs, bm, cm, dt, a_log, d_skip):
    bsz, seqlen = xs.shape[0], xs.shape[1]
    nc, q = seqlen // SSD_CHUNK, SSD_CHUNK
    g, e, p, n = SSD_GROUPS, SSD_HEADS_PER_GROUP, SSD_HEAD_DIM, SSD_STATE
    a = -jnp.exp(a_log.astype(jnp.float32)).reshape(g, e)
    x_c = xs.reshape(bsz, nc, q, g, e, p)
    b_c = bm.reshape(bsz, nc, q, g, n)
    c_c = cm.reshape(bsz, nc, q, g, n)
    dt_c = dt.reshape(bsz, nc, q, g, e)
    acum = jnp.cumsum(jnp.moveaxis(dt_c * a, 2, -1), axis=-1)
    causal = jnp.tril(jnp.ones((q, q), dtype=bool))
    seg = acum[..., :, None] - acum[..., None, :]
    l_dec = jnp.exp(jnp.where(causal, seg, -jnp.inf))
    xdt = x_c * dt_c[..., None]
    cb = jnp.einsum("bclgn,bcsgn->bcgls", c_c, b_c)
    y_diag = jnp.einsum("bcgels,bcsgep->bclgep", cb[:, :, :, None] * l_dec, xdt)
    dec_to_end = jnp.moveaxis(jnp.exp(acum[..., -1:] - acum), -1, 2)
    states = jnp.einsum("bcsgn,bcsgep->bcgepn", b_c, xdt * dec_to_end[..., None])
    chunk_decay = jnp.exp(acum[..., -1])

    def step(s, inp):
        st, dec = inp
        return s * dec[..., None, None] + st, s

    s0 = jnp.zeros((bsz, g, e, p, n), jnp.float32)
    _, s_in = lax.scan(step, s0, (jnp.moveaxis(states, 1, 0).astype(jnp.float32),
                                  jnp.moveaxis(chunk_decay, 1, 0)))
    s_in = jnp.moveaxis(s_in, 0, 1)
    dec_from_start = jnp.moveaxis(jnp.exp(acum), -1, 2)
    y_off = jnp.einsum("bclgn,bcgepn->bclgep", c_c, s_in) * dec_from_start[..., None]
    y = (y_diag + y_off).reshape(bsz, seqlen, SSD_HEADS, p)
    return y + d_skip.astype(jnp.float32)[:, None] * xs


def hgrn2_mixer(q_raw, f_raw, i_val, lb):
    bsz, seqlen = q_raw.shape[0], q_raw.shape[1]
    nc, c = seqlen // HG_CHUNK, HG_CHUNK
    qf = jax.nn.silu(q_raw)
    fg = lb + (1.0 - lb) * jax.nn.sigmoid(f_raw.astype(jnp.float32))
    kf = 1.0 - fg
    gl = jnp.log(fg)

    def to_chunks(t):
        return t.reshape(bsz, nc, c, t.shape[2], t.shape[3]).transpose(1, 0, 3, 2, 4)

    causal = jnp.tril(jnp.ones((c, c), dtype=bool))[:, :, None]

    def step(s, inp):
        qc, kc, vc, gc = inp
        bcum = jnp.cumsum(gc, axis=2)
        o_inter = jnp.einsum("bhqk,bhkv->bhqv", qc * jnp.exp(bcum), s)
        seg = bcum[:, :, :, None, :] - bcum[:, :, None, :, :]
        dec = jnp.exp(jnp.where(causal, seg, -jnp.inf))
        att = jnp.einsum("bhik,bhijk->bhij", qc, dec * kc[:, :, None, :, :])
        o_intra = jnp.einsum("bhij,bhjv->bhiv", att, vc)
        b_last = bcum[:, :, -1:, :]
        s_new = s * jnp.exp(b_last[:, :, 0, :])[..., None] + jnp.einsum(
            "bhjk,bhjv->bhkv", kc * jnp.exp(b_last - bcum), vc)
        return s_new, (o_inter + o_intra).astype(jnp.float32)

    s0 = jnp.zeros((bsz, HG_HEADS, HG_K, HG_V), jnp.float32)
    _, o = lax.scan(step, s0, (to_chunks(qf), to_chunks(kf), to_chunks(i_val), to_chunks(gl)))
    return o.transpose(1, 0, 3, 2, 4).reshape(bsz, seqlen, HG_HEADS, HG_V)


def setup_inputs(seed: int = 0) -> dict:
    key = jax.random.key(seed)
    ks = jax.random.split(key, 24)
    f32 = jnp.float32

    def nrm(k, shape, scale):
        return jax.random.normal(k, shape, f32) * scale

    def gain(k, shape):
        return 1.0 + 0.02 * jax.random.normal(k, shape, f32)

    dt = jnp.exp(jax.random.uniform(ks[5], (DEPTH, SSD_HEADS), f32)
                 * (math.log(0.1) - math.log(0.001)) + math.log(0.001))
    return {
        "x": nrm(ks[0], (BATCH, SEQ, D_MODEL), 1.0),
        "mem": nrm(ks[1], (BATCH, MEM_LEN, D_MODEL), 1.0),
        "norm_mix_w": gain(ks[2], (DEPTH, D_MODEL)),
        "w_in": nrm(ks[3], (DEPTH, D_MODEL, N_IN), D_MODEL ** -0.5),
        "conv_w": nrm(ks[4], (DEPTH, SSD_CONV, SSD_CONV_CH), SSD_CONV ** -0.5),
        "conv_b": nrm(ks[6], (DEPTH, SSD_CONV_CH), 0.02),
        "dt_bias": dt + jnp.log(-jnp.expm1(-dt)),
        "a_log": jnp.log(jax.random.uniform(ks[7], (DEPTH, SSD_HEADS), f32, 1.0, 16.0)),
        "d_skip": 1.0 + 0.1 * jax.random.normal(ks[8], (DEPTH, SSD_HEADS), f32),
        "ssd_norm_w": gain(ks[9], (DEPTH, SSD_DIM)),
        "hg_lower_bounds": nrm(ks[10], (DEPTH + 1, HG_KDIM), 0.5),
        "hg_norm_w": gain(ks[11], (DEPTH, HG_V)),
        "w_out": nrm(ks[12], (DEPTH, D_MIX, D_MODEL), D_MIX ** -0.5),
        "norm_xa_w": gain(ks[13], (DEPTH, D_MODEL)),
        "norm_mem_w": gain(ks[14], (DEPTH, D_MODEL)),
        "xa_wq": nrm(ks[15], (DEPTH, D_MODEL, D_MODEL), D_MODEL ** -0.5),
        "xa_wkv": nrm(ks[16], (DEPTH, D_MODEL, 2 * D_MODEL), D_MODEL ** -0.5),
        "xa_wo": nrm(ks[17], (DEPTH, D_MODEL, D_MODEL), D_MODEL ** -0.5),
        "norm_ffn_w": gain(ks[18], (DEPTH, D_MODEL)),
        "ffn_w_gate": nrm(ks[19], (DEPTH, D_MODEL, FFN_DIM), D_MODEL ** -0.5),
        "ffn_w_up": nrm(ks[20], (DEPTH, D_MODEL, FFN_DIM), D_MODEL ** -0.5),
        "ffn_w_down": nrm(ks[21], (DEPTH, FFN_DIM, D_MODEL), FFN_DIM ** -0.5),
        "norm_final_w": gain(ks[22], (D_MODEL,)),
    }


def reference(x, mem, norm_mix_w, w_in, conv_w, conv_b, dt_bias, a_log, d_skip, ssd_norm_w,
              hg_lower_bounds, hg_norm_w, w_out, norm_xa_w, norm_mem_w, xa_wq, xa_wkv, xa_wo,
              norm_ffn_w, ffn_w_gate, ffn_w_up, ffn_w_down, norm_final_w):
    bsz, seqlen, _ = x.shape
    lb_all = jnp.cumsum(jax.nn.softmax(hg_lower_bounds.astype(jnp.float32), axis=0), axis=0)
    s1 = SSD_DIM
    s2 = s1 + SSD_CONV_CH
    s3 = s2 + SSD_HEADS
    s4 = s3 + HG_KDIM
    s5 = s4 + HG_KDIM
    s6 = s5 + HG_VDIM
    for l in range(DEPTH):
        h = rmsnorm(x, norm_mix_w[l])
        proj = h @ w_in[l]
        z, xbc, dt_raw, hq, hf, hi, hgate = jnp.split(proj, [s1, s2, s3, s4, s5, s6], axis=-1)
        xbc = jax.nn.silu(causal_depthwise_conv(xbc, conv_w[l], conv_b[l]))
        xs, bm, cm = jnp.split(xbc, [SSD_DIM, SSD_DIM + SSD_GROUPS * SSD_STATE], axis=-1)
        dt = jax.nn.softplus((dt_raw + dt_bias[l]).astype(jnp.float32))
        y_a = ssd_mixer(xs.reshape(bsz, seqlen, SSD_HEADS, SSD_HEAD_DIM),
                        bm.reshape(bsz, seqlen, SSD_GROUPS, SSD_STATE),
                        cm.reshape(bsz, seqlen, SSD_GROUPS, SSD_STATE),
                        dt, a_log[l], d_skip[l])
        yz = (y_a.reshape(bsz, seqlen, SSD_DIM) * jax.nn.silu(z)).reshape(
            bsz, seqlen, SSD_GROUPS, SSD_DIM // SSD_GROUPS)
        y_a = rmsnorm(yz, ssd_norm_w[l].reshape(SSD_GROUPS, -1)).reshape(bsz, seqlen, SSD_DIM)
        o_b = hgrn2_mixer(hq.reshape(bsz, seqlen, HG_HEADS, HG_K),
                          hf.reshape(bsz, seqlen, HG_HEADS, HG_K),
                          hi.reshape(bsz, seqlen, HG_HEADS, HG_V),
                          lb_all[l].reshape(HG_HEADS, HG_K))
        o_b = rmsnorm(o_b, hg_norm_w[l]) * jax.nn.silu(hgate.reshape(bsz, seqlen, HG_HEADS, HG_V))
        mixed = jnp.concatenate([y_a, o_b.reshape(bsz, seqlen, HG_VDIM)], axis=-1).astype(x.dtype)
        x = x + mixed @ w_out[l]
        h = rmsnorm(x, norm_xa_w[l])
        m = rmsnorm(mem, norm_mem_w[l])
        qx = (h @ xa_wq[l]).reshape(bsz, seqlen, XA_HEADS, XA_HEAD_DIM)
        km, vm = jnp.split(m @ xa_wkv[l], 2, axis=-1)
        km = km.reshape(bsz, MEM_LEN, XA_HEADS, XA_HEAD_DIM)
        vm = vm.reshape(bsz, MEM_LEN, XA_HEADS, XA_HEAD_DIM)
        sc = jnp.einsum("bqhd,bkhd->bhqk", qx, km, preferred_element_type=jnp.float32)
        pr = jax.nn.softmax(sc * (XA_HEAD_DIM ** -0.5), axis=-1).astype(vm.dtype)
        ox = jnp.einsum("bhqk,bkhd->bqhd", pr, vm).reshape(bsz, seqlen, D_MODEL)
        x = x + ox @ xa_wo[l]
        h = rmsnorm(x, norm_ffn_w[l])
        x = x + (jax.nn.silu(h @ ffn_w_gate[l]) * (h @ ffn_w_up[l])) @ ffn_w_down[l]
    return rmsnorm(x, norm_final_w)
```

```python
import functools

import numpy as np
import jax
import jax.numpy as jnp
from jax import lax
from jax.experimental import pallas as pl
from jax.experimental.pallas import tpu as pltpu

F32 = jnp.float32
BF16 = jnp.bfloat16
EPS = 1e-6

D_MODEL = 1024
SSD_HEADS = 16
SSD_HEAD_DIM = 64
SSD_DIM = SSD_HEADS * SSD_HEAD_DIM
SSD_GROUPS = 2
SSD_STATE = 128
SSD_CONV = 4
SSD_CHUNK = 128
SSD_CONV_CH = SSD_DIM + 2 * SSD_GROUPS * SSD_STATE
HG_HEADS = 8
HG_K = 128
HG_V = 128
HG_KDIM = HG_HEADS * HG_K
HG_VDIM = HG_HEADS * HG_V
XA_HEADS = 4
XA_HEAD_DIM = D_MODEL // XA_HEADS

LANES = 128
SUBLANES = 8
VMEM_LIMIT_BYTES = 56 * 1024 * 1024

HG_CHUNK = 128
IN_PROJ_ROWS = 512
ATTN_ROWS = 512
FFN_ROWS = 512
FFN_COLS = 256


def _cparams(sem):
    return pltpu.CompilerParams(dimension_semantics=sem, vmem_limit_bytes=VMEM_LIMIT_BYTES)


def _const_spec(shape):
    zeros = (0,) * len(shape)
    return pl.BlockSpec(shape, lambda *_: zeros)


def _dot(a, b):
    return jnp.dot(a, b, preferred_element_type=F32)


def _dot_nt(a, b):
    return lax.dot_general(a, b, (((1,), (1,)), ((), ())), preferred_element_type=F32)


def _dot_tn(a, b):
    return lax.dot_general(a, b, (((0,), (0,)), ((), ())), preferred_element_type=F32)


def _sigmoid(x):
    return 1.0 / (1.0 + jnp.exp(-x))


def _silu(x):
    return x * _sigmoid(x)


def _rms(x, w):
    ms = jnp.mean(x * x, axis=-1, keepdims=True)
    return x * lax.rsqrt(ms + EPS) * w


def _split_hi_lo(x):
    hi = x.astype(BF16)
    lo = (x - hi.astype(F32)).astype(BF16)
    return hi, lo


def _mem_kv_kernel(mem_ref, nw_ref, wkv_ref, k_ref, v_ref):
    m = _rms(mem_ref[0], nw_ref[...]).astype(BF16)
    kv = _dot(m, wkv_ref[...])
    k_ref[0] = kv[:, :D_MODEL].astype(BF16)
    v_ref[0] = kv[:, D_MODEL:].astype(BF16)


def _mem_kv(mem, norm_w, wkv):
    bsz, mlen, d = mem.shape
    out = jax.ShapeDtypeStruct((bsz, mlen, d), BF16)
    return pl.pallas_call(
        _mem_kv_kernel,
        grid=(bsz,),
        in_specs=[pl.BlockSpec((1, mlen, d), lambda b: (b, 0, 0)),
                  _const_spec((1, d)), _const_spec((d, 2 * d))],
        out_specs=[pl.BlockSpec((1, mlen, d), lambda b: (b, 0, 0))] * 2,
        out_shape=[out, out],
        compiler_params=_cparams(("arbitrary",)),
        name="mem_kv",
    )(mem, norm_w, wkv)


def _in_proj_kernel(x_ref, nw_ref, w_ref, wdt_ref,
                    z_ref, xbc_ref, dt_ref, hq_ref, hf_ref, hi_ref, hg_ref):
    h = _rms(x_ref[...], nw_ref[...]).astype(BF16)
    col = 0
    for ref in (z_ref, xbc_ref, hq_ref, hf_ref, hi_ref, hg_ref):
        n = ref.shape[-1]
        ref[...] = _dot(h, w_ref[:, col:col + n]).astype(ref.dtype)
        col += n
    dt_ref[...] = _dot(h, wdt_ref[...])


def _in_proj(x2d, norm_w, w_main, w_dt):
    t, d = x2d.shape
    tm = min(IN_PROJ_ROWS, t)
    widths = (SSD_DIM, SSD_CONV_CH, LANES, HG_KDIM, HG_KDIM, HG_VDIM, HG_VDIM)
    dtypes = (BF16, BF16, F32, BF16, F32, BF16, BF16)
    return pl.pallas_call(
        _in_proj_kernel,
        grid=(t // tm,),
        in_specs=[pl.BlockSpec((tm, d), lambda i: (i, 0)),
                  _const_spec((1, d)), _const_spec(w_main.shape), _const_spec(w_dt.shape)],
        out_specs=[pl.BlockSpec((tm, n), lambda i: (i, 0)) for n in widths],
        out_shape=[jax.ShapeDtypeStruct((t, n), dt) for n, dt in zip(widths, dtypes)],
        compiler_params=_cparams(("arbitrary",)),
        name="in_proj",
    )(x2d, norm_w, w_main, w_dt)


def _ssd_kernel(z_ref, xbc_ref, dt_ref, convw_ref, convb_ref, dtb_ref, alog_ref,
                dskip_ref, nw_ref, expand_ref, o_ref,
                ext_ref, state_ref, y_ref, *, chunks_per_batch):
    q = SSD_CHUNK
    gw = SSD_DIM // SSD_GROUPS
    step = pl.program_id(0)
    first = step % chunks_per_batch == 0

    @pl.when(first)
    def _():
        ext_ref[0:SUBLANES, :] = jnp.zeros((SUBLANES, SSD_CONV_CH), F32)
        state_ref[...] = jnp.zeros_like(state_ref)

    @pl.when(jnp.logical_not(first))
    def _():
        ext_ref[0:SUBLANES, :] = ext_ref[q:q + SUBLANES, :]

    ext_ref[SUBLANES:SUBLANES + q, :] = xbc_ref[...].astype(F32)

    acc = convb_ref[...] + convw_ref[0:1, :] * ext_ref[pl.ds(SUBLANES - 3, q), :]
    for k in range(1, SSD_CONV):
        acc = acc + convw_ref[k:k + 1, :] * ext_ref[pl.ds(SUBLANES - 3 + k, q), :]
    xbc = _silu(acc)
    xs = xbc[:, :SSD_DIM]
    b_all = xbc[:, SSD_DIM:SSD_DIM + SSD_GROUPS * SSD_STATE]
    c_all = xbc[:, SSD_DIM + SSD_GROUPS * SSD_STATE:]

    dtr = dt_ref[...] + dtb_ref[...]
    dt = jnp.maximum(dtr, 0.0) + jnp.log(1.0 + jnp.exp(-jnp.abs(dtr)))
    da = dt * (-jnp.exp(alog_ref[...]))
    row = lax.broadcasted_iota(jnp.int32, (q, q), 0)
    colm = lax.broadcasted_iota(jnp.int32, (q, q), 1)
    causal = row >= colm
    tri = causal.astype(BF16)
    da_hi, da_lo = _split_hi_lo(da)
    da_lo2 = (da - da_hi.astype(F32) - da_lo.astype(F32)).astype(BF16)
    acum = _dot(tri, da_hi) + _dot(tri, da_lo) + _dot(tri, da_lo2)
    acum_t = acum.T
    a_last = acum[q - 1:q, :]
    dec_from_start = jnp.exp(acum)
    dec_to_end = jnp.exp(a_last - acum)
    chunk_decay = jnp.broadcast_to(jnp.exp(a_last), (SUBLANES, LANES))

    stack = jnp.concatenate([dt, dec_from_start, dec_to_end, chunk_decay], axis=0)
    s_hi, s_lo = _split_hi_lo(stack)
    wide = _dot(s_hi, expand_ref[...]) + _dot(s_lo, expand_ref[...])
    dt_w = wide[0:q]
    dfs_w = wide[q:2 * q]
    dte_w = wide[2 * q:3 * q]
    cd_w = wide[3 * q:3 * q + 1]

    xdt = xs * dt_w
    xw = (xdt * dte_w).astype(BF16)

    lane = lax.broadcasted_iota(jnp.int32, (q, LANES), 1)
    left = lane < SSD_HEAD_DIM
    heads_per_group = SSD_HEADS // SSD_GROUPS
    for g in range(SSD_GROUPS):
        b_g = b_all[:, g * SSD_STATE:(g + 1) * SSD_STATE].astype(BF16)
        c_g = c_all[:, g * SSD_STATE:(g + 1) * SSD_STATE].astype(BF16)
        cb = _dot_nt(c_g, b_g)
        gsl = slice(g * gw, (g + 1) * gw)
        s_in = state_ref[g]
        y_ref[:, gsl] = _dot(c_g, s_in.astype(BF16)) * dfs_w[:, gsl]
        state_ref[g] = s_in * cd_w[:, gsl] + _dot_tn(b_g, xw[:, gsl])
        for pair in range(heads_per_group // 2):
            h0 = g * heads_per_group + 2 * pair
            ms = []
            for h in (h0, h0 + 1):
                seg = acum[:, h:h + 1] - acum_t[h:h + 1, :]
                l_dec = jnp.exp(jnp.where(causal, seg, -jnp.inf))
                ms.append((cb * l_dec).astype(BF16))
            lhs = jnp.concatenate(ms, axis=1)
            psl = slice(h0 * SSD_HEAD_DIM, (h0 + 2) * SSD_HEAD_DIM)
            xp = xdt[:, psl]
            rhs = jnp.concatenate([jnp.where(left, xp, 0.0), jnp.where(left, 0.0, xp)],
                                  axis=0).astype(BF16)
            y_ref[:, psl] += _dot(lhs, rhs)

    y = y_ref[...] + dskip_ref[...] * xs
    yz = y * _silu(z_ref[...].astype(F32))
    for g in range(SSD_GROUPS):
        gsl = slice(g * gw, (g + 1) * gw)
        o_ref[:, gsl] = _rms(yz[:, gsl], nw_ref[:, gsl]).astype(o_ref.dtype)


def _ssd(z, xbc, dt, conv_w, conv_b, dt_bias, a_log, d_skip_w, norm_w, expand, chunks_per_batch):
    t = z.shape[0]
    q = SSD_CHUNK
    kern = functools.partial(_ssd_kernel, chunks_per_batch=chunks_per_batch)
    return pl.pallas_call(
        kern,
        grid=(t // q,),
        in_specs=[pl.BlockSpec((q, SSD_DIM), lambda i: (i, 0)),
                  pl.BlockSpec((q, SSD_CONV_CH), lambda i: (i, 0)),
                  pl.BlockSpec((q, LANES), lambda i: (i, 0)),
                  _const_spec(conv_w.shape), _const_spec(conv_b.shape),
                  _const_spec(dt_bias.shape), _const_spec(a_log.shape),
                  _const_spec(d_skip_w.shape), _const_spec(norm_w.shape),
                  _const_spec(expand.shape)],
        out_specs=pl.BlockSpec((q, SSD_DIM), lambda i: (i, 0)),
        out_shape=jax.ShapeDtypeStruct((t, SSD_DIM), BF16),
        scratch_shapes=[pltpu.VMEM((q + SUBLANES, SSD_CONV_CH), F32),
                        pltpu.VMEM((SSD_GROUPS, SSD_STATE, SSD_DIM // SSD_GROUPS), F32),
                        pltpu.VMEM((q, SSD_DIM), F32)],
        compiler_params=_cparams(("arbitrary",)),
        name="ssd",
    )(z, xbc, dt, conv_w, conv_b, dt_bias, a_log, d_skip_w, norm_w, expand)


def _hgrn2_tables(c):
    levels = []
    s = c // 2
    while s >= 1:
        levels.append(s)
        s //= 2
    idx = np.arange(c)
    blocks, masks = [], []
    for s in levels:
        mid = (idx // (2 * s)) * (2 * s) + s
        upper = idx >= mid
        a = np.zeros((c, c), np.float32)
        for i in range(c):
            if upper[i]:
                a[i, mid[i]:i + 1] = 1.0
            else:
                a[i, i + 1:mid[i]] = 1.0
        blocks.append(a)
        same = (idx[:, None] // (2 * s)) == (idx[None, :] // (2 * s))
        masks.append((same & upper[:, None] & (~upper)[None, :]).astype(np.float32))
    masks.append(np.eye(c, dtype=np.float32))
    tri = np.tril(np.ones((c, c), np.float32))
    blocks.append(tri)
    blocks.append(np.triu(np.ones((c, c), np.float32), 1))
    sums = np.stack(blocks)
    sums2 = np.concatenate([sums, sums], axis=2)
    return len(levels), jnp.asarray(sums2, BF16), jnp.asarray(np.stack(masks), F32)


def _hgrn2_kernel(hq_ref, hf_ref, hi_ref, hg_ref, lbp_ref, nw_ref, sums_ref, mask_ref,
                  o_ref, state_ref, g2_ref, q_ref, k_ref, *, chunks_per_batch, n_levels):
    c = hq_ref.shape[0]
    step = pl.program_id(0)

    @pl.when(step % chunks_per_batch == 0)
    def _():
        state_ref[...] = jnp.zeros_like(state_ref)

    lbp = lbp_ref[...]
    e = jnp.exp(lbp - jnp.max(lbp, axis=0, keepdims=True))
    lb = e[0:1, :] / jnp.sum(e, axis=0, keepdims=True)

    fg = lb + (1.0 - lb) * _sigmoid(hf_ref[...])
    gl = jnp.log(fg)
    g_hi, g_lo = _split_hi_lo(gl)
    g2_ref[0:c, :] = g_hi
    g2_ref[c:2 * c, :] = g_lo
    k_ref[...] = 1.0 - fg
    q_ref[...] = _silu(hq_ref[...].astype(F32))

    for h in range(HG_HEADS):
        sl = slice(h * HG_K, (h + 1) * HG_K)
        g2 = g2_ref[:, sl]
        qh = q_ref[:, sl]
        kh = k_ref[:, sl]
        vh = hi_ref[:, sl]
        att = mask_ref[n_levels] * _dot_nt(qh.astype(BF16), kh.astype(BF16))
        for lvl in range(n_levels):
            ex = jnp.exp(_dot(sums_ref[lvl], g2))
            p = _dot_nt((qh * ex).astype(BF16), (kh * ex).astype(BF16))
            att = att + jnp.where(mask_ref[lvl] != 0.0, p, 0.0)
        b = _dot(sums_ref[n_levels], g2)
        to_end = _dot(sums_ref[n_levels + 1], g2)
        s_t = state_ref[h]
        o = _dot_nt((qh * jnp.exp(b)).astype(BF16), s_t.astype(BF16))
        o = o + _dot(att.astype(BF16), vh)
        ke = (kh * jnp.exp(to_end)).astype(BF16)
        state_ref[h] = s_t * jnp.exp(b[c - 1:c, :]) + _dot_tn(vh, ke)
        gate = _silu(hg_ref[:, sl].astype(F32))
        o_ref[:, sl] = (_rms(o, nw_ref[...]) * gate).astype(o_ref.dtype)


def _hgrn2(hq, hf, hi, hgate, lb_params, norm_w, chunks_per_batch):
    t = hq.shape[0]
    c = HG_CHUNK
    n_levels, sums, masks = _hgrn2_tables(c)
    kern = functools.partial(_hgrn2_kernel, chunks_per_batch=chunks_per_batch, n_levels=n_levels)
    blk = lambda n: pl.BlockSpec((c, n), lambda i: (i, 0))
    return pl.pallas_call(
        kern,
        grid=(t // c,),
        in_specs=[blk(HG_KDIM), blk(HG_KDIM), blk(HG_VDIM), blk(HG_VDIM),
                  _const_spec(lb_params.shape), _const_spec(norm_w.shape),
                  _const_spec(sums.shape), _const_spec(masks.shape)],
        out_specs=blk(HG_VDIM),
        out_shape=jax.ShapeDtypeStruct((t, HG_VDIM), BF16),
        scratch_shapes=[pltpu.VMEM((HG_HEADS, HG_V, HG_K), F32),
                        pltpu.VMEM((2 * c, HG_KDIM), BF16),
                        pltpu.VMEM((c, HG_KDIM), F32),
                        pltpu.VMEM((c, HG_KDIM), F32)],
        compiler_params=_cparams(("arbitrary",)),
        name="hgrn2",
    )(hq, hf, hi, hgate, lb_params, norm_w, sums, masks)


def _attn_kernel(x_ref, ya_ref, ob_ref, wout_ref, nw_ref, wq_ref, km_ref, vm_ref, wo_ref,
                 o_ref, ox_ref):
    x1 = x_ref[...] + _dot(ya_ref[...], wout_ref[0:SSD_DIM, :]) \
        + _dot(ob_ref[...], wout_ref[SSD_DIM:, :])
    h = _rms(x1, nw_ref[...]).astype(BF16)
    qx = (_dot(h, wq_ref[...]) * (XA_HEAD_DIM ** -0.5)).astype(BF16)
    for hd in range(XA_HEADS):
        sl = slice(hd * XA_HEAD_DIM, (hd + 1) * XA_HEAD_DIM)
        sc = _dot_nt(qx[:, sl], km_ref[0, :, sl])
        p = jnp.exp(sc - jnp.max(sc, axis=-1, keepdims=True))
        denom = jnp.sum(p, axis=-1, keepdims=True)
        ox_ref[:, sl] = (_dot(p.astype(BF16), vm_ref[0, :, sl]) / denom).astype(BF16)
    o_ref[...] = x1 + _dot(ox_ref[...], wo_ref[...])


def _attn(x2d, ya, ob, w_out, norm_w, wq, km, vm, wo, seqlen):
    t, d = x2d.shape
    tm = min(ATTN_ROWS, seqlen)
    tpb = seqlen // tm
    mlen = km.shape[1]
    row = lambda n: pl.BlockSpec((tm, n), lambda i: (i, 0))
    mem = pl.BlockSpec((1, mlen, d), lambda i: (i // tpb, 0, 0))
    return pl.pallas_call(
        _attn_kernel,
        grid=(t // tm,),
        in_specs=[row(d), row(SSD_DIM), row(HG_VDIM), _const_spec(w_out.shape),
                  _const_spec(norm_w.shape), _const_spec(wq.shape), mem, mem,
                  _const_spec(wo.shape)],
        out_specs=row(d),
        out_shape=jax.ShapeDtypeStruct((t, d), F32),
        scratch_shapes=[pltpu.VMEM((tm, d), BF16)],
        compiler_params=_cparams(("arbitrary",)),
        name="attn",
    )(x2d, ya, ob, w_out, norm_w, wq, km, vm, wo)


def _ffn_kernel(x_ref, nw_ref, wg_ref, wu_ref, wd_ref, nf_ref, o_ref, acc_ref):
    x = x_ref[...]
    h = _rms(x, nw_ref[...]).astype(BF16)
    acc_ref[...] = x
    n_cols = wg_ref.shape[1]
    for c0 in range(0, n_cols, FFN_COLS):
        g = _dot(h, wg_ref[:, c0:c0 + FFN_COLS])
        u = _dot(h, wu_ref[:, c0:c0 + FFN_COLS])
        a = (_silu(g) * u).astype(BF16)
        acc_ref[...] += _dot(a, wd_ref[c0:c0 + FFN_COLS, :])
    o_ref[...] = _rms(acc_ref[...], nf_ref[...])


def _ffn(x2d, norm_w, wg, wu, wd, norm_final):
    t, d = x2d.shape
    tm = min(FFN_ROWS, t)
    row = pl.BlockSpec((tm, d), lambda i: (i, 0))
    return pl.pallas_call(
        _ffn_kernel,
        grid=(t // tm,),
        in_specs=[row, _const_spec(norm_w.shape), _const_spec(wg.shape), _const_spec(wu.shape),
                  _const_spec(wd.shape), _const_spec(norm_final.shape)],
        out_specs=row,
        out_shape=jax.ShapeDtypeStruct((t, d), F32),
        scratch_shapes=[pltpu.VMEM((tm, d), F32)],
        compiler_params=_cparams(("arbitrary",)),
        name="ffn",
    )(x2d, norm_w, wg, wu, wd, norm_final)


def _pad_lanes(v):
    return jnp.pad(v.astype(F32), (0, LANES - v.shape[0])).reshape(1, LANES)


def kernel(x, mem, norm_mix_w, w_in, conv_w, conv_b, dt_bias, a_log, d_skip, ssd_norm_w,
           hg_lower_bounds, hg_norm_w, w_out, norm_xa_w, norm_mem_w, xa_wq, xa_wkv, xa_wo,
           norm_ffn_w, ffn_w_gate, ffn_w_up, ffn_w_down, norm_final_w):
    bsz, seqlen, d = x.shape
    assert d == D_MODEL and norm_mix_w.shape[0] == 1, "single-layer block of width 1024"
    assert hg_lower_bounds.shape[0] == 2
    assert seqlen % SSD_CHUNK == 0 and seqlen % HG_CHUNK == 0
    t = bsz * seqlen
    x2d = x.reshape(t, d)
    row = lambda v: v.reshape(1, -1).astype(F32)

    s1 = SSD_DIM
    s2 = s1 + SSD_CONV_CH
    s3 = s2 + SSD_HEADS
    w = w_in[0]
    w_main = jnp.concatenate([w[:, :s2], w[:, s3:]], axis=1).astype(BF16)
    w_dt = jnp.pad(w[:, s2:s3], ((0, 0), (0, LANES - SSD_HEADS))).astype(BF16)

    z, xbc, dt, hq, hf, hi, hgate = _in_proj(x2d, row(norm_mix_w[0]), w_main, w_dt)

    expand = (np.arange(LANES)[:, None] == (np.arange(SSD_DIM)[None, :] // SSD_HEAD_DIM))
    expand = jnp.asarray(expand, BF16)
    d_skip_w = jnp.repeat(d_skip[0].astype(F32), SSD_HEAD_DIM).reshape(1, SSD_DIM)
    y_a = _ssd(z, xbc, dt, conv_w[0].astype(F32), row(conv_b[0]), _pad_lanes(dt_bias[0]),
               _pad_lanes(a_log[0]), d_skip_w, row(ssd_norm_w[0]), expand,
               seqlen // SSD_CHUNK)
    o_b = _hgrn2(hq, hf, hi, hgate, hg_lower_bounds.astype(F32), row(hg_norm_w[0]),
                 seqlen // HG_CHUNK)

    km, vm = _mem_kv(mem, row(norm_mem_w[0]), xa_wkv[0].astype(BF16))
    x2 = _attn(x2d, y_a, o_b, w_out[0].astype(BF16), row(norm_xa_w[0]), xa_wq[0].astype(BF16),
               km, vm, xa_wo[0].astype(BF16), seqlen)
    out = _ffn(x2, row(norm_ffn_w[0]), ffn_w_gate[0].astype(BF16), ffn_w_up[0].astype(BF16),
               ffn_w_down[0].astype(BF16), row(norm_final_w))
    return out.reshape(bsz, seqlen, d)
```

```python
import functools

import numpy as np
import jax
import jax.numpy as jnp
from jax import lax
from jax.experimental import pallas as pl
from jax.experimental.pallas import tpu as pltpu

F32 = jnp.float32
BF16 = jnp.bfloat16
EPS = 1e-6

D_MODEL = 1024
SSD_HEADS = 16
SSD_HEAD_DIM = 64
SSD_DIM = SSD_HEADS * SSD_HEAD_DIM
SSD_GROUPS = 2
SSD_STATE = 128
SSD_CONV = 4
SSD_CHUNK = 128
SSD_CONV_CH = SSD_DIM + 2 * SSD_GROUPS * SSD_STATE
HG_HEADS = 8
HG_K = 128
HG_V = 128
HG_KDIM = HG_HEADS * HG_K
HG_VDIM = HG_HEADS * HG_V
XA_HEADS = 4
XA_HEAD_DIM = D_MODEL // XA_HEADS

LANES = 128
SUBLANES = 8
VMEM_LIMIT_BYTES = 56 * 1024 * 1024

HG_CHUNK = 128
IN_PROJ_ROWS = 512
ATTN_ROWS = 512
FFN_ROWS = 512
FFN_COLS = 256


def _cparams(sem):
    return pltpu.CompilerParams(dimension_semantics=sem, vmem_limit_bytes=VMEM_LIMIT_BYTES)


def _const_spec(shape):
    zeros = (0,) * len(shape)
    return pl.BlockSpec(shape, lambda *_: zeros)


def _dot(a, b):
    return jnp.dot(a, b, preferred_element_type=F32)


def _dot_nt(a, b):
    return lax.dot_general(a, b, (((1,), (1,)), ((), ())), preferred_element_type=F32)


def _dot_tn(a, b):
    return lax.dot_general(a, b, (((0,), (0,)), ((), ())), preferred_element_type=F32)


def _sigmoid(x):
    return 1.0 / (1.0 + jnp.exp(-x))


def _silu(x):
    return x * _sigmoid(x)


def _rms(x, w):
    ms = jnp.mean(x * x, axis=-1, keepdims=True)
    return x * lax.rsqrt(ms + EPS) * w


def _split_hi_lo(x):
    hi = x.astype(BF16)
    lo = (x - hi.astype(F32)).astype(BF16)
    return hi, lo


def _mem_kv_kernel(mem_ref, nw_ref, wkv_ref, k_ref, v_ref):
    m = _rms(mem_ref[0], nw_ref[...]).astype(BF16)
    kv = _dot(m, wkv_ref[...])
    k_ref[0] = kv[:, :D_MODEL].astype(BF16)
    v_ref[0] = kv[:, D_MODEL:].astype(BF16)


def _mem_kv(mem, norm_w, wkv):
    bsz, mlen, d = mem.shape
    out = jax.ShapeDtypeStruct((bsz, mlen, d), BF16)
    return pl.pallas_call(
        _mem_kv_kernel,
        grid=(bsz,),
        in_specs=[pl.BlockSpec((1, mlen, d), lambda b: (b, 0, 0)),
                  _const_spec((1, d)), _const_spec((d, 2 * d))],
        out_specs=[pl.BlockSpec((1, mlen, d), lambda b: (b, 0, 0))] * 2,
        out_shape=[out, out],
        compiler_params=_cparams(("arbitrary",)),
        name="mem_kv",
    )(mem, norm_w, wkv)


def _in_proj_kernel(x_ref, nw_ref, w_ref, wdt_ref,
                    z_ref, xbc_ref, dt_ref, hq_ref, hf_ref, hi_ref, hg_ref):
    h = _rms(x_ref[...], nw_ref[...]).astype(BF16)
    col = 0
    for ref in (z_ref, xbc_ref, hq_ref, hf_ref, hi_ref, hg_ref):
        n = ref.shape[-1]
        ref[...] = _dot(h, w_ref[:, col:col + n]).astype(ref.dtype)
        col += n
    dt_ref[...] = _dot(h, wdt_ref[...])


def _in_proj(x2d, norm_w, w_main, w_dt):
    t, d = x2d.shape
    tm = min(IN_PROJ_ROWS, t)
    widths = (SSD_DIM, SSD_CONV_CH, LANES, HG_KDIM, HG_KDIM, HG_VDIM, HG_VDIM)
    dtypes = (BF16, BF16, F32, BF16, F32, BF16, BF16)
    return pl.pallas_call(
        _in_proj_kernel,
        grid=(t // tm,),
        in_specs=[pl.BlockSpec((tm, d), lambda i: (i, 0)),
                  _const_spec((1, d)), _const_spec(w_main.shape), _const_spec(w_dt.shape)],
        out_specs=[pl.BlockSpec((tm, n), lambda i: (i, 0)) for n in widths],
        out_shape=[jax.ShapeDtypeStruct((t, n), dt) for n, dt in zip(widths, dtypes)],
        compiler_params=_cparams(("arbitrary",)),
        name="in_proj",
    )(x2d, norm_w, w_main, w_dt)


def _ssd_kernel(z_ref, xbc_ref, dt_ref, convw_ref, convb_ref, dtb_ref, alog_ref,
                dskip_ref, nw_ref, expand_ref, o_ref,
                ext_ref, state_ref, y_ref, *, chunks_per_batch):
    q = SSD_CHUNK
    gw = SSD_DIM // SSD_GROUPS
    step = pl.program_id(0)
    first = step % chunks_per_batch == 0

    @pl.when(first)
    def _():
        ext_ref[0:SUBLANES, :] = jnp.zeros((SUBLANES, SSD_CONV_CH), F32)
        state_ref[...] = jnp.zeros_like(state_ref)

    @pl.when(jnp.logical_not(first))
    def _():
        ext_ref[0:SUBLANES, :] = ext_ref[q:q + SUBLANES, :]

    ext_ref[SUBLANES:SUBLANES + q, :] = xbc_ref[...].astype(F32)

    acc = convb_ref[...] + convw_ref[0:1, :] * ext_ref[pl.ds(SUBLANES - 3, q), :]
    for k in range(1, SSD_CONV):
        acc = acc + convw_ref[k:k + 1, :] * ext_ref[pl.ds(SUBLANES - 3 + k, q), :]
    xbc = _silu(acc)
    xs = xbc[:, :SSD_DIM]
    b_all = xbc[:, SSD_DIM:SSD_DIM + SSD_GROUPS * SSD_STATE]
    c_all = xbc[:, SSD_DIM + SSD_GROUPS * SSD_STATE:]

    dtr = dt_ref[...] + dtb_ref[...]
    dt = jnp.maximum(dtr, 0.0) + jnp.log(1.0 + jnp.exp(-jnp.abs(dtr)))
    da = dt * (-jnp.exp(alog_ref[...]))
    row = lax.broadcasted_iota(jnp.int32, (q, q), 0)
    colm = lax.broadcasted_iota(jnp.int32, (q, q), 1)
    causal = row >= colm
    tri = causal.astype(BF16)
    da_hi, da_lo = _split_hi_lo(da)
    da_lo2 = (da - da_hi.astype(F32) - da_lo.astype(F32)).astype(BF16)
    acum = _dot(tri, da_hi) + _dot(tri, da_lo) + _dot(tri, da_lo2)
    acum_t = acum.T
    a_last = acum[q - 1:q, :]
    dec_from_start = jnp.exp(acum)
    dec_to_end = jnp.exp(a_last - acum)
    chunk_decay = jnp.broadcast_to(jnp.exp(a_last), (SUBLANES, LANES))

    stack = jnp.concatenate([dt, dec_from_start, dec_to_end, chunk_decay], axis=0)
    s_hi, s_lo = _split_hi_lo(stack)
    wide = _dot(s_hi, expand_ref[...]) + _dot(s_lo, expand_ref[...])
    dt_w = wide[0:q]
    dfs_w = wide[q:2 * q]
    dte_w = wide[2 * q:3 * q]
    cd_w = wide[3 * q:3 * q + 1]

    xdt = xs * dt_w
    xw = (xdt * dte_w).astype(BF16)

    lane = lax.broadcasted_iota(jnp.int32, (q, LANES), 1)
    left = lane < SSD_HEAD_DIM
    heads_per_group = SSD_HEADS // SSD_GROUPS
    for g in range(SSD_GROUPS):
        b_g = b_all[:, g * SSD_STATE:(g + 1) * SSD_STATE].astype(BF16)
        c_g = c_all[:, g * SSD_STATE:(g + 1) * SSD_STATE].astype(BF16)
        cb = _dot_nt(c_g, b_g)
        gsl = slice(g * gw, (g + 1) * gw)
        s_in = state_ref[g]
        y_ref[:, gsl] = _dot(c_g, s_in.astype(BF16)) * dfs_w[:, gsl]
        state_ref[g] = s_in * cd_w[:, gsl] + _dot_tn(b_g, xw[:, gsl])
        for pair in range(heads_per_group // 2):
            h0 = g * heads_per_group + 2 * pair
            ms = []
            for h in (h0, h0 + 1):
                seg = acum[:, h:h + 1] - acum_t[h:h + 1, :]
                l_dec = jnp.exp(jnp.where(causal, seg, -jnp.inf))
                ms.append((cb * l_dec).astype(BF16))
            lhs = jnp.concatenate(ms, axis=1)
            psl = slice(h0 * SSD_HEAD_DIM, (h0 + 2) * SSD_HEAD_DIM)
            xp = xdt[:, psl]
            rhs = jnp.concatenate([jnp.where(left, xp, 0.0), jnp.where(left, 0.0, xp)],
                                  axis=0).astype(BF16)
            y_ref[:, psl] += _dot(lhs, rhs)

    y = y_ref[...] + dskip_ref[...] * xs
    yz = y * _silu(z_ref[...].astype(F32))
    for g in range(SSD_GROUPS):
        gsl = slice(g * gw, (g + 1) * gw)
        o_ref[:, gsl] = _rms(yz[:, gsl], nw_ref[:, gsl]).astype(o_ref.dtype)


def _ssd(z, xbc, dt, conv_w, conv_b, dt_bias, a_log, d_skip_w, norm_w, expand, chunks_per_batch):
    t = z.shape[0]
    q = SSD_CHUNK
    kern = functools.partial(_ssd_kernel, chunks_per_batch=chunks_per_batch)
    return pl.pallas_call(
        kern,
        grid=(t // q,),
        in_specs=[pl.BlockSpec((q, SSD_DIM), lambda i: (i, 0)),
                  pl.BlockSpec((q, SSD_CONV_CH), lambda i: (i, 0)),
                  pl.BlockSpec((q, LANES), lambda i: (i, 0)),
                  _const_spec(conv_w.shape), _const_spec(conv_b.shape),
                  _const_spec(dt_bias.shape), _const_spec(a_log.shape),
                  _const_spec(d_skip_w.shape), _const_spec(norm_w.shape),
                  _const_spec(expand.shape)],
        out_specs=pl.BlockSpec((q, SSD_DIM), lambda i: (i, 0)),
        out_shape=jax.ShapeDtypeStruct((t, SSD_DIM), BF16),
        scratch_shapes=[pltpu.VMEM((q + SUBLANES, SSD_CONV_CH), F32),
                        pltpu.VMEM((SSD_GROUPS, SSD_STATE, SSD_DIM // SSD_GROUPS), F32),
                        pltpu.VMEM((q, SSD_DIM), F32)],
        compiler_params=_cparams(("arbitrary",)),
        name="ssd",
    )(z, xbc, dt, conv_w, conv_b, dt_bias, a_log, d_skip_w, norm_w, expand)


def _hgrn2_tables(c):
    levels = []
    s = c // 2
    while s >= 1:
        levels.append(s)
        s //= 2
    idx = np.arange(c)
    blocks, masks = [], []
    for s in levels:
        mid = (idx // (2 * s)) * (2 * s) + s
        upper = idx >= mid
        a = np.zeros((c, c), np.float32)
        for i in range(c):
            if upper[i]:
                a[i, mid[i]:i + 1] = 1.0
            else:
                a[i, i + 1:mid[i]] = 1.0
        blocks.append(a)
        same = (idx[:, None] // (2 * s)) == (idx[None, :] // (2 * s))
        masks.append((same & upper[:, None] & (~upper)[None, :]).astype(np.float32))
    masks.append(np.eye(c, dtype=np.float32))
    blocks.append(np.tril(np.ones((c, c), np.float32)))
    blocks.append(np.triu(np.ones((c, c), np.float32), 1))
    sums = np.stack(blocks)
    sums2 = np.concatenate([sums, sums], axis=2)
    masks = np.stack(masks)
    masks2 = np.concatenate([masks, masks], axis=2)
    return len(levels), jnp.asarray(sums2, BF16), jnp.asarray(masks2, F32)


def _hgrn2_kernel(hq_ref, hf_ref, hi_ref, hg_ref, lbp_ref, nw_ref, sums_ref, mask_ref,
                  o_ref, state_ref, g2_ref, q_ref, k_ref, ql_ref, kl_ref, qe_ref, ke_ref,
                  *, chunks_per_batch, n_levels):
    c = hq_ref.shape[0]
    step = pl.program_id(0)

    @pl.when(step % chunks_per_batch == 0)
    def _():
        state_ref[...] = jnp.zeros_like(state_ref)

    lbp = lbp_ref[...]
    e = jnp.exp(lbp - jnp.max(lbp, axis=0, keepdims=True))
    lb = e[0:1, :] / jnp.sum(e, axis=0, keepdims=True)

    fg = lb + (1.0 - lb) * _sigmoid(hf_ref[...])
    g_hi, g_lo = _split_hi_lo(jnp.log2(fg))
    g2_ref[0:c, :] = g_hi
    g2_ref[c:2 * c, :] = g_lo
    k_ref[...] = 1.0 - fg
    q_ref[...] = _silu(hq_ref[...].astype(F32))

    for lvl in range(n_levels):
        ex = jnp.exp2(_dot(sums_ref[lvl], g2_ref[...]))
        ql_ref[lvl] = (q_ref[...] * ex).astype(BF16)
        kl_ref[lvl] = (k_ref[...] * ex).astype(BF16)
    ql_ref[n_levels] = q_ref[...].astype(BF16)
    kl_ref[n_levels] = k_ref[...].astype(BF16)
    b = _dot(sums_ref[n_levels], g2_ref[...])
    qe_ref[...] = (q_ref[...] * jnp.exp2(b)).astype(BF16)
    state_decay = jnp.exp2(b[c - 1:c, :])
    to_end = _dot(sums_ref[n_levels + 1], g2_ref[...])
    ke_ref[...] = (k_ref[...] * jnp.exp2(to_end)).astype(BF16)

    zero = jnp.zeros((c, HG_K), BF16)
    for pair in range(HG_HEADS // 2):
        psl = slice(2 * pair * HG_K, (2 * pair + 2) * HG_K)
        att = jnp.zeros((c, 2 * c), F32)
        for lvl in range(n_levels + 1):
            kp = kl_ref[lvl, :, psl]
            kbd = jnp.concatenate(
                [jnp.concatenate([kp[:, :HG_K], zero], axis=1),
                 jnp.concatenate([zero, kp[:, HG_K:]], axis=1)], axis=0)
            att = att + _dot_nt(ql_ref[lvl, :, psl], kbd) * mask_ref[lvl]
        for sub in range(2):
            h = 2 * pair + sub
            sl = slice(h * HG_K, (h + 1) * HG_K)
            vh = hi_ref[:, sl]
            s_t = state_ref[h]
            o = _dot_nt(qe_ref[:, sl], s_t.astype(BF16))
            o = o + _dot(att[:, sub * c:(sub + 1) * c].astype(BF16), vh)
            state_ref[h] = s_t * state_decay[:, sl] + _dot_tn(vh, ke_ref[:, sl])
            gate = _silu(hg_ref[:, sl].astype(F32))
            o_ref[:, sl] = (_rms(o, nw_ref[...]) * gate).astype(o_ref.dtype)


def _hgrn2(hq, hf, hi, hgate, lb_params, norm_w, chunks_per_batch):
    t = hq.shape[0]
    c = HG_CHUNK
    n_levels, sums, masks = _hgrn2_tables(c)
    kern = functools.partial(_hgrn2_kernel, chunks_per_batch=chunks_per_batch, n_levels=n_levels)
    blk = lambda n: pl.BlockSpec((c, n), lambda i: (i, 0))
    return pl.pallas_call(
        kern,
        grid=(t // c,),
        in_specs=[blk(HG_KDIM), blk(HG_KDIM), blk(HG_VDIM), blk(HG_VDIM),
                  _const_spec(lb_params.shape), _const_spec(norm_w.shape),
                  _const_spec(sums.shape), _const_spec(masks.shape)],
        out_specs=blk(HG_VDIM),
        out_shape=jax.ShapeDtypeStruct((t, HG_VDIM), BF16),
        scratch_shapes=[pltpu.VMEM((HG_HEADS, HG_V, HG_K), F32),
                        pltpu.VMEM((2 * c, HG_KDIM), BF16),
                        pltpu.VMEM((c, HG_KDIM), F32),
                        pltpu.VMEM((c, HG_KDIM), F32),
                        pltpu.VMEM((n_levels + 1, c, HG_KDIM), BF16),
                        pltpu.VMEM((n_levels + 1, c, HG_KDIM), BF16),
                        pltpu.VMEM((c, HG_KDIM), BF16),
                        pltpu.VMEM((c, HG_KDIM), BF16)],
        compiler_params=_cparams(("arbitrary",)),
        name="hgrn2",
    )(hq, hf, hi, hgate, lb_params, norm_w, sums, masks)


def _attn_kernel(x_ref, ya_ref, ob_ref, wout_ref, nw_ref, wq_ref, km_ref, vm_ref, wo_ref,
                 o_ref, ox_ref):
    x1 = x_ref[...] + _dot(ya_ref[...], wout_ref[0:SSD_DIM, :]) \
        + _dot(ob_ref[...], wout_ref[SSD_DIM:, :])
    h = _rms(x1, nw_ref[...]).astype(BF16)
    qx = (_dot(h, wq_ref[...]) * (XA_HEAD_DIM ** -0.5)).astype(BF16)
    for hd in range(XA_HEADS):
        sl = slice(hd * XA_HEAD_DIM, (hd + 1) * XA_HEAD_DIM)
        sc = _dot_nt(qx[:, sl], km_ref[0, :, sl])
        p = jnp.exp(sc - jnp.max(sc, axis=-1, keepdims=True))
        denom = jnp.sum(p, axis=-1, keepdims=True)
        ox_ref[:, sl] = (_dot(p.astype(BF16), vm_ref[0, :, sl]) / denom).astype(BF16)
    o_ref[...] = x1 + _dot(ox_ref[...], wo_ref[...])


def _attn(x2d, ya, ob, w_out, norm_w, wq, km, vm, wo, seqlen):
    t, d = x2d.shape
    tm = min(ATTN_ROWS, seqlen)
    tpb = seqlen // tm
    mlen = km.shape[1]
    row = lambda n: pl.BlockSpec((tm, n), lambda i: (i, 0))
    mem = pl.BlockSpec((1, mlen, d), lambda i: (i // tpb, 0, 0))
    return pl.pallas_call(
        _attn_kernel,
        grid=(t // tm,),
        in_specs=[row(d), row(SSD_DIM), row(HG_VDIM), _const_spec(w_out.shape),
                  _const_spec(norm_w.shape), _const_spec(wq.shape), mem, mem,
                  _const_spec(wo.shape)],
        out_specs=row(d),
        out_shape=jax.ShapeDtypeStruct((t, d), F32),
        scratch_shapes=[pltpu.VMEM((tm, d), BF16)],
        compiler_params=_cparams(("arbitrary",)),
        name="attn",
    )(x2d, ya, ob, w_out, norm_w, wq, km, vm, wo)


def _ffn_kernel(x_ref, nw_ref, wg_ref, wu_ref, wd_ref, nf_ref, o_ref, acc_ref):
    x = x_ref[...]
    h = _rms(x, nw_ref[...]).astype(BF16)
    acc_ref[...] = x
    n_cols = wg_ref.shape[1]
    for c0 in range(0, n_cols, FFN_COLS):
        g = _dot(h, wg_ref[:, c0:c0 + FFN_COLS])
        u = _dot(h, wu_ref[:, c0:c0 + FFN_COLS])
        a = (_silu(g) * u).astype(BF16)
        acc_ref[...] += _dot(a, wd_ref[c0:c0 + FFN_COLS, :])
    o_ref[...] = _rms(acc_ref[...], nf_ref[...])


def _ffn(x2d, norm_w, wg, wu, wd, norm_final):
    t, d = x2d.shape
    tm = min(FFN_ROWS, t)
    row = pl.BlockSpec((tm, d), lambda i: (i, 0))
    return pl.pallas_call(
        _ffn_kernel,
        grid=(t // tm,),
        in_specs=[row, _const_spec(norm_w.shape), _const_spec(wg.shape), _const_spec(wu.shape),
                  _const_spec(wd.shape), _const_spec(norm_final.shape)],
        out_specs=row,
        out_shape=jax.ShapeDtypeStruct((t, d), F32),
        scratch_shapes=[pltpu.VMEM((tm, d), F32)],
        compiler_params=_cparams(("arbitrary",)),
        name="ffn",
    )(x2d, norm_w, wg, wu, wd, norm_final)


def _pad_lanes(v):
    return jnp.pad(v.astype(F32), (0, LANES - v.shape[0])).reshape(1, LANES)


def kernel(x, mem, norm_mix_w, w_in, conv_w, conv_b, dt_bias, a_log, d_skip, ssd_norm_w,
           hg_lower_bounds, hg_norm_w, w_out, norm_xa_w, norm_mem_w, xa_wq, xa_wkv, xa_wo,
           norm_ffn_w, ffn_w_gate, ffn_w_up, ffn_w_down, norm_final_w):
    bsz, seqlen, d = x.shape
    assert d == D_MODEL and norm_mix_w.shape[0] == 1, "single-layer block of width 1024"
    assert hg_lower_bounds.shape[0] == 2
    assert seqlen % SSD_CHUNK == 0 and seqlen % HG_CHUNK == 0
    t = bsz * seqlen
    x2d = x.reshape(t, d)
    row = lambda v: v.reshape(1, -1).astype(F32)

    s1 = SSD_DIM
    s2 = s1 + SSD_CONV_CH
    s3 = s2 + SSD_HEADS
    w = w_in[0]
    w_main = jnp.concatenate([w[:, :s2], w[:, s3:]], axis=1).astype(BF16)
    w_dt = jnp.pad(w[:, s2:s3], ((0, 0), (0, LANES - SSD_HEADS))).astype(BF16)

    z, xbc, dt, hq, hf, hi, hgate = _in_proj(x2d, row(norm_mix_w[0]), w_main, w_dt)

    expand = (np.arange(LANES)[:, None] == (np.arange(SSD_DIM)[None, :] // SSD_HEAD_DIM))
    expand = jnp.asarray(expand, BF16)
    d_skip_w = jnp.repeat(d_skip[0].astype(F32), SSD_HEAD_DIM).reshape(1, SSD_DIM)
    y_a = _ssd(z, xbc, dt, conv_w[0].astype(F32), row(conv_b[0]), _pad_lanes(dt_bias[0]),
               _pad_lanes(a_log[0]), d_skip_w, row(ssd_norm_w[0]), expand,
               seqlen // SSD_CHUNK)
    o_b = _hgrn2(hq, hf, hi, hgate, hg_lower_bounds.astype(F32), row(hg_norm_w[0]),
                 seqlen // HG_CHUNK)

    km, vm = _mem_kv(mem, row(norm_mem_w[0]), xa_wkv[0].astype(BF16))
    x2 = _attn(x2d, y_a, o_b, w_out[0].astype(BF16), row(norm_xa_w[0]), xa_wq[0].astype(BF16),
               km, vm, xa_wo[0].astype(BF16), seqlen)
    out = _ffn(x2, row(norm_ffn_w[0]), ffn_w_gate[0].astype(BF16), ffn_w_up[0].astype(BF16),
               ffn_w_down[0].astype(BF16), row(norm_final_w))
    return out.reshape(bsz, seqlen, d)
```

```python
import functools

import numpy as np
import jax
import jax.numpy as jnp
from jax import lax
from jax.experimental import pallas as pl
from jax.experimental.pallas import tpu as pltpu

F32 = jnp.float32
BF16 = jnp.bfloat16
EPS = 1e-6

D_MODEL = 1024
SSD_HEADS = 16
SSD_HEAD_DIM = 64
SSD_DIM = SSD_HEADS * SSD_HEAD_DIM
SSD_GROUPS = 2
SSD_STATE = 128
SSD_CONV = 4
SSD_CHUNK = 128
SSD_CONV_CH = SSD_DIM + 2 * SSD_GROUPS * SSD_STATE
HG_HEADS = 8
HG_K = 128
HG_V = 128
HG_KDIM = HG_HEADS * HG_K
HG_VDIM = HG_HEADS * HG_V
XA_HEADS = 4
XA_HEAD_DIM = D_MODEL // XA_HEADS

LANES = 128
SUBLANES = 8
CONV_TAIL = 16
VMEM_LIMIT_BYTES = 56 * 1024 * 1024

HG_CHUNK = 128
IN_PROJ_ROWS = 512
ATTN_ROWS = 512
FFN_ROWS = 512
FFN_COLS = 256


def _cparams(sem):
    return pltpu.CompilerParams(dimension_semantics=sem, vmem_limit_bytes=VMEM_LIMIT_BYTES)


def _const_spec(shape):
    zeros = (0,) * len(shape)
    return pl.BlockSpec(shape, lambda *_: zeros)


def _dot(a, b):
    return jnp.dot(a, b, preferred_element_type=F32)


def _dot_nt(a, b):
    return lax.dot_general(a, b, (((1,), (1,)), ((), ())), preferred_element_type=F32)


def _dot_tn(a, b):
    return lax.dot_general(a, b, (((0,), (0,)), ((), ())), preferred_element_type=F32)


def _sigmoid(x):
    return 1.0 / (1.0 + jnp.exp(-x))


def _silu(x):
    return x * _sigmoid(x)


def _rms(x, w):
    ms = jnp.mean(x * x, axis=-1, keepdims=True)
    return x * lax.rsqrt(ms + EPS) * w


def _split_hi_lo(x):
    hi = x.astype(BF16)
    lo = (x - hi.astype(F32)).astype(BF16)
    return hi, lo


def _mem_kv_kernel(mem_ref, nw_ref, wkv_ref, k_ref, v_ref):
    m = _rms(mem_ref[0], nw_ref[...]).astype(BF16)
    kv = _dot(m, wkv_ref[...])
    k_ref[0] = kv[:, :D_MODEL].astype(BF16)
    v_ref[0] = kv[:, D_MODEL:].astype(BF16)


def _mem_kv(mem, norm_w, wkv):
    bsz, mlen, d = mem.shape
    out = jax.ShapeDtypeStruct((bsz, mlen, d), BF16)
    return pl.pallas_call(
        _mem_kv_kernel,
        grid=(bsz,),
        in_specs=[pl.BlockSpec((1, mlen, d), lambda b: (b, 0, 0)),
                  _const_spec((1, d)), _const_spec((d, 2 * d))],
        out_specs=[pl.BlockSpec((1, mlen, d), lambda b: (b, 0, 0))] * 2,
        out_shape=[out, out],
        compiler_params=_cparams(("arbitrary",)),
        name="mem_kv",
    )(mem, norm_w, wkv)


def _in_proj_kernel(x_ref, nw_ref, w_ref, wdt_ref,
                    z_ref, xbc_ref, dt_ref, hq_ref, hf_ref, hi_ref, hg_ref):
    h = _rms(x_ref[...], nw_ref[...]).astype(BF16)
    col = 0
    for ref in (z_ref, xbc_ref, hq_ref, hf_ref, hi_ref, hg_ref):
        n = ref.shape[-1]
        ref[...] = _dot(h, w_ref[:, col:col + n]).astype(ref.dtype)
        col += n
    dt_ref[...] = _dot(h, wdt_ref[...])


def _in_proj(x2d, norm_w, w_main, w_dt):
    t, d = x2d.shape
    tm = min(IN_PROJ_ROWS, t)
    widths = (SSD_DIM, SSD_CONV_CH, LANES, HG_KDIM, HG_KDIM, HG_VDIM, HG_VDIM)
    dtypes = (BF16, BF16, F32, BF16, F32, BF16, BF16)
    return pl.pallas_call(
        _in_proj_kernel,
        grid=(t // tm,),
        in_specs=[pl.BlockSpec((tm, d), lambda i: (i, 0)),
                  _const_spec((1, d)), _const_spec(w_main.shape), _const_spec(w_dt.shape)],
        out_specs=[pl.BlockSpec((tm, n), lambda i: (i, 0)) for n in widths],
        out_shape=[jax.ShapeDtypeStruct((t, n), dt) for n, dt in zip(widths, dtypes)],
        compiler_params=_cparams(("arbitrary",)),
        name="in_proj",
    )(x2d, norm_w, w_main, w_dt)


def _ssd_kernel(z_ref, xbc_ref, dt_ref, convw_ref, convb_ref, dtb_ref, alog_ref,
                dskip_ref, nw_ref, expand_ref, shift_ref, o_ref,
                ext_ref, state_ref, y_ref, *, chunks_per_batch):
    q = SSD_CHUNK
    gw = SSD_DIM // SSD_GROUPS
    step = pl.program_id(0)
    first = step % chunks_per_batch == 0

    @pl.when(first)
    def _():
        ext_ref[0:CONV_TAIL, :] = jnp.zeros((CONV_TAIL, SSD_CONV_CH), BF16)
        state_ref[...] = jnp.zeros_like(state_ref)

    @pl.when(jnp.logical_not(first))
    def _():
        ext_ref[0:CONV_TAIL, :] = ext_ref[q:q + CONV_TAIL, :]

    ext_ref[CONV_TAIL:CONV_TAIL + q, :] = xbc_ref[...]

    shifted = _dot(shift_ref[...], ext_ref[...])
    acc = convb_ref[...] + convw_ref[SSD_CONV - 1:SSD_CONV, :] * xbc_ref[...].astype(F32)
    for k in range(SSD_CONV - 1):
        acc = acc + convw_ref[k:k + 1, :] * shifted[k * q:(k + 1) * q]
    xbc = _silu(acc)
    xs = xbc[:, :SSD_DIM]
    b_all = xbc[:, SSD_DIM:SSD_DIM + SSD_GROUPS * SSD_STATE]
    c_all = xbc[:, SSD_DIM + SSD_GROUPS * SSD_STATE:]

    dtr = dt_ref[...] + dtb_ref[...]
    dt = jnp.maximum(dtr, 0.0) + jnp.log(1.0 + jnp.exp(-jnp.abs(dtr)))
    da = dt * (-jnp.exp(alog_ref[...]))
    row = lax.broadcasted_iota(jnp.int32, (q, q), 0)
    colm = lax.broadcasted_iota(jnp.int32, (q, q), 1)
    causal = row >= colm
    tri = causal.astype(BF16)
    da_hi, da_lo = _split_hi_lo(da)
    da_lo2 = (da - da_hi.astype(F32) - da_lo.astype(F32)).astype(BF16)
    acum = _dot(tri, da_hi) + _dot(tri, da_lo) + _dot(tri, da_lo2)
    acum_t = acum.T
    a_last = acum[q - 1:q, :]
    dec_from_start = jnp.exp(acum)
    dec_to_end = jnp.exp(a_last - acum)
    chunk_decay = jnp.broadcast_to(jnp.exp(a_last), (SUBLANES, LANES))

    stack = jnp.concatenate([dt, dec_from_start, dec_to_end, chunk_decay], axis=0)
    s_hi, s_lo = _split_hi_lo(stack)
    wide = _dot(s_hi, expand_ref[...]) + _dot(s_lo, expand_ref[...])
    dt_w = wide[0:q]
    dfs_w = wide[q:2 * q]
    dte_w = wide[2 * q:3 * q]
    cd_w = wide[3 * q:3 * q + 1]

    xdt = xs * dt_w
    xw = (xdt * dte_w).astype(BF16)

    lane = lax.broadcasted_iota(jnp.int32, (q, LANES), 1)
    left = lane < SSD_HEAD_DIM
    heads_per_group = SSD_HEADS // SSD_GROUPS
    for g in range(SSD_GROUPS):
        b_g = b_all[:, g * SSD_STATE:(g + 1) * SSD_STATE].astype(BF16)
        c_g = c_all[:, g * SSD_STATE:(g + 1) * SSD_STATE].astype(BF16)
        cb = _dot_nt(c_g, b_g)
        gsl = slice(g * gw, (g + 1) * gw)
        s_in = state_ref[g]
        y_ref[:, gsl] = _dot(c_g, s_in.astype(BF16)) * dfs_w[:, gsl]
        state_ref[g] = s_in * cd_w[:, gsl] + _dot_tn(b_g, xw[:, gsl])
        for pair in range(heads_per_group // 2):
            h0 = g * heads_per_group + 2 * pair
            ms = []
            for h in (h0, h0 + 1):
                seg = acum[:, h:h + 1] - acum_t[h:h + 1, :]
                l_dec = jnp.exp(jnp.where(causal, seg, -jnp.inf))
                ms.append((cb * l_dec).astype(BF16))
            lhs = jnp.concatenate(ms, axis=1)
            psl = slice(h0 * SSD_HEAD_DIM, (h0 + 2) * SSD_HEAD_DIM)
            xp = xdt[:, psl]
            rhs = jnp.concatenate([jnp.where(left, xp, 0.0), jnp.where(left, 0.0, xp)],
                                  axis=0).astype(BF16)
            y_ref[:, psl] += _dot(lhs, rhs)

    y = y_ref[...] + dskip_ref[...] * xs
    yz = y * _silu(z_ref[...].astype(F32))
    for g in range(SSD_GROUPS):
        gsl = slice(g * gw, (g + 1) * gw)
        o_ref[:, gsl] = _rms(yz[:, gsl], nw_ref[:, gsl]).astype(o_ref.dtype)


def _conv_shift_matrix(q):
    s = np.zeros(((SSD_CONV - 1) * q, CONV_TAIL + q), np.float32)
    for k in range(SSD_CONV - 1):
        for t in range(q):
            s[k * q + t, CONV_TAIL + t - (SSD_CONV - 1) + k] = 1.0
    return jnp.asarray(s, BF16)


def _ssd(z, xbc, dt, conv_w, conv_b, dt_bias, a_log, d_skip_w, norm_w, expand, chunks_per_batch):
    t = z.shape[0]
    q = SSD_CHUNK
    shift = _conv_shift_matrix(q)
    kern = functools.partial(_ssd_kernel, chunks_per_batch=chunks_per_batch)
    return pl.pallas_call(
        kern,
        grid=(t // q,),
        in_specs=[pl.BlockSpec((q, SSD_DIM), lambda i: (i, 0)),
                  pl.BlockSpec((q, SSD_CONV_CH), lambda i: (i, 0)),
                  pl.BlockSpec((q, LANES), lambda i: (i, 0)),
                  _const_spec(conv_w.shape), _const_spec(conv_b.shape),
                  _const_spec(dt_bias.shape), _const_spec(a_log.shape),
                  _const_spec(d_skip_w.shape), _const_spec(norm_w.shape),
                  _const_spec(expand.shape), _const_spec(shift.shape)],
        out_specs=pl.BlockSpec((q, SSD_DIM), lambda i: (i, 0)),
        out_shape=jax.ShapeDtypeStruct((t, SSD_DIM), BF16),
        scratch_shapes=[pltpu.VMEM((CONV_TAIL + q, SSD_CONV_CH), BF16),
                        pltpu.VMEM((SSD_GROUPS, SSD_STATE, SSD_DIM // SSD_GROUPS), F32),
                        pltpu.VMEM((q, SSD_DIM), F32)],
        compiler_params=_cparams(("arbitrary",)),
        name="ssd",
    )(z, xbc, dt, conv_w, conv_b, dt_bias, a_log, d_skip_w, norm_w, expand, shift)


HG_VPU_MIN_HALF = 8


def _hgrn2_levels(c):
    levels = []
    s = c // 2
    while s >= 1:
        levels.append(s)
        s //= 2
    return levels


def _hgrn2_tables(c):
    idx = np.arange(c)
    blocks = [np.tril(np.ones((c, c), np.float32))]
    masks = []
    for s in _hgrn2_levels(c):
        mid = (idx // (2 * s)) * (2 * s) + s
        upper = idx >= mid
        if s < HG_VPU_MIN_HALF:
            a = np.zeros((c, c), np.float32)
            for i in range(c):
                if upper[i]:
                    a[i, mid[i]:i + 1] = 1.0
                else:
                    a[i, i + 1:mid[i]] = 1.0
            blocks.append(a)
        same = (idx[:, None] // (2 * s)) == (idx[None, :] // (2 * s))
        masks.append((same & upper[:, None] & (~upper)[None, :]).astype(np.float32))
    masks.append(np.eye(c, dtype=np.float32))
    sums = np.concatenate(blocks, axis=0)
    sums2 = np.concatenate([sums, sums], axis=1)
    masks = np.stack(masks)
    masks2 = np.concatenate([masks, masks], axis=2)
    return jnp.asarray(sums2, BF16), jnp.asarray(masks2, F32)


def _hgrn2_kernel(hq_ref, hf_ref, hi_ref, hg_ref, lbp_ref, nw_ref, sums_ref, mask_ref,
                  o_ref, state_ref, g2_ref, q_ref, k_ref, b_ref, ql_ref, klt_ref, qe_ref, ke_ref,
                  *, chunks_per_batch):
    c = hq_ref.shape[0]
    levels = _hgrn2_levels(c)
    n_levels = len(levels)
    step = pl.program_id(0)

    @pl.when(step % chunks_per_batch == 0)
    def _():
        state_ref[...] = jnp.zeros_like(state_ref)

    lbp = lbp_ref[...]
    e = jnp.exp(lbp - jnp.max(lbp, axis=0, keepdims=True))
    lb = e[0:1, :] / jnp.sum(e, axis=0, keepdims=True)

    fg = lb + (1.0 - lb) * _sigmoid(hf_ref[...])
    g_hi, g_lo = _split_hi_lo(jnp.log2(fg))
    g2_ref[0:c, :] = g_hi
    g2_ref[c:2 * c, :] = g_lo
    k_ref[...] = 1.0 - fg
    q_ref[...] = _silu(hq_ref[...].astype(F32))

    fine = _dot(sums_ref[...], g2_ref[...])
    b_ref[...] = fine[0:c]
    b_last = b_ref[c - 1:c, :]
    qe_ref[...] = (q_ref[...] * jnp.exp2(b_ref[...])).astype(BF16)
    ke_ref[...] = (k_ref[...] * jnp.exp2(b_last - b_ref[...])).astype(BF16)
    state_decay = jnp.exp2(b_last)

    n_fine = 0
    for lvl, s in enumerate(levels):
        if s >= HG_VPU_MIN_HALF:
            q_rows, k_rows = [], []
            for r0 in range(0, c, 2 * s):
                lo = slice(r0, r0 + s)
                up = slice(r0 + s, r0 + 2 * s)
                ref = b_ref[r0 + s - 1:r0 + s, :]
                zeros = jnp.zeros((s, HG_KDIM), F32)
                q_rows += [zeros, q_ref[up, :] * jnp.exp2(b_ref[up, :] - ref)]
                k_rows += [k_ref[lo, :] * jnp.exp2(ref - b_ref[lo, :]), zeros]
            ql = jnp.concatenate(q_rows, axis=0)
            kl = jnp.concatenate(k_rows, axis=0)
        else:
            n_fine += 1
            ex = jnp.exp2(fine[n_fine * c:(n_fine + 1) * c])
            ql = q_ref[...] * ex
            kl = k_ref[...] * ex
        ql_ref[lvl] = ql.astype(BF16)
        klt_ref[lvl] = kl.T
    ql_ref[n_levels] = q_ref[...].astype(BF16)
    klt_ref[n_levels] = k_ref[...].T

    zero = jnp.zeros((HG_K, c), BF16)
    for pair in range(HG_HEADS // 2):
        psl = slice(2 * pair * HG_K, (2 * pair + 2) * HG_K)
        att = None
        for lvl in range(n_levels + 1):
            kt = klt_ref[lvl, psl, :].astype(BF16)
            kbd = jnp.concatenate(
                [jnp.concatenate([kt[:HG_K], zero], axis=1),
                 jnp.concatenate([zero, kt[HG_K:]], axis=1)], axis=0)
            p = _dot(ql_ref[lvl, :, psl], kbd)
            att = p if lvl == 0 else att + p * mask_ref[lvl]
        for sub in range(2):
            h = 2 * pair + sub
            sl = slice(h * HG_K, (h + 1) * HG_K)
            vh = hi_ref[:, sl]
            s_t = state_ref[h]
            o = _dot_nt(qe_ref[:, sl], s_t.astype(BF16))
            o = o + _dot(att[:, sub * c:(sub + 1) * c].astype(BF16), vh)
            state_ref[h] = s_t * state_decay[:, sl] + _dot_tn(vh, ke_ref[:, sl])
            gate = _silu(hg_ref[:, sl].astype(F32))
            o_ref[:, sl] = (_rms(o, nw_ref[...]) * gate).astype(o_ref.dtype)


def _hgrn2(hq, hf, hi, hgate, lb_params, norm_w, chunks_per_batch):
    t = hq.shape[0]
    c = HG_CHUNK
    n_levels = len(_hgrn2_levels(c))
    sums, masks = _hgrn2_tables(c)
    kern = functools.partial(_hgrn2_kernel, chunks_per_batch=chunks_per_batch)
    blk = lambda n: pl.BlockSpec((c, n), lambda i: (i, 0))
    return pl.pallas_call(
        kern,
        grid=(t // c,),
        in_specs=[blk(HG_KDIM), blk(HG_KDIM), blk(HG_VDIM), blk(HG_VDIM),
                  _const_spec(lb_params.shape), _const_spec(norm_w.shape),
                  _const_spec(sums.shape), _const_spec(masks.shape)],
        out_specs=blk(HG_VDIM),
        out_shape=jax.ShapeDtypeStruct((t, HG_VDIM), BF16),
        scratch_shapes=[pltpu.VMEM((HG_HEADS, HG_V, HG_K), F32),
                        pltpu.VMEM((2 * c, HG_KDIM), BF16),
                        pltpu.VMEM((c, HG_KDIM), F32),
                        pltpu.VMEM((c, HG_KDIM), F32),
                        pltpu.VMEM((c, HG_KDIM), F32),
                        pltpu.VMEM((n_levels + 1, c, HG_KDIM), BF16),
                        pltpu.VMEM((n_levels + 1, HG_KDIM, c), F32),
                        pltpu.VMEM((c, HG_KDIM), BF16),
                        pltpu.VMEM((c, HG_KDIM), BF16)],
        compiler_params=_cparams(("arbitrary",)),
        name="hgrn2",
    )(hq, hf, hi, hgate, lb_params, norm_w, sums, masks)


def _attn_kernel(x_ref, ya_ref, ob_ref, wout_ref, nw_ref, wq_ref, km_ref, vm_ref, wo_ref,
                 o_ref, ox_ref):
    x1 = x_ref[...] + _dot(ya_ref[...], wout_ref[0:SSD_DIM, :]) \
        + _dot(ob_ref[...], wout_ref[SSD_DIM:, :])
    h = _rms(x1, nw_ref[...]).astype(BF16)
    qx = (_dot(h, wq_ref[...]) * (XA_HEAD_DIM ** -0.5)).astype(BF16)
    for hd in range(XA_HEADS):
        sl = slice(hd * XA_HEAD_DIM, (hd + 1) * XA_HEAD_DIM)
        sc = _dot_nt(qx[:, sl], km_ref[0, :, sl])
        p = jnp.exp(sc - jnp.max(sc, axis=-1, keepdims=True))
        denom = jnp.sum(p, axis=-1, keepdims=True)
        ox_ref[:, sl] = (_dot(p.astype(BF16), vm_ref[0, :, sl]) / denom).astype(BF16)
    o_ref[...] = x1 + _dot(ox_ref[...], wo_ref[...])


def _attn(x2d, ya, ob, w_out, norm_w, wq, km, vm, wo, seqlen):
    t, d = x2d.shape
    tm = min(ATTN_ROWS, seqlen)
    tpb = seqlen // tm
    mlen = km.shape[1]
    row = lambda n: pl.BlockSpec((tm, n), lambda i: (i, 0))
    mem = pl.BlockSpec((1, mlen, d), lambda i: (i // tpb, 0, 0))
    return pl.pallas_call(
        _attn_kernel,
        grid=(t // tm,),
        in_specs=[row(d), row(SSD_DIM), row(HG_VDIM), _const_spec(w_out.shape),
                  _const_spec(norm_w.shape), _const_spec(wq.shape), mem, mem,
                  _const_spec(wo.shape)],
        out_specs=row(d),
        out_shape=jax.ShapeDtypeStruct((t, d), F32),
        scratch_shapes=[pltpu.VMEM((tm, d), BF16)],
        compiler_params=_cparams(("arbitrary",)),
        name="attn",
    )(x2d, ya, ob, w_out, norm_w, wq, km, vm, wo)


def _ffn_kernel(x_ref, nw_ref, wg_ref, wu_ref, wd_ref, nf_ref, o_ref, acc_ref):
    x = x_ref[...]
    h = _rms(x, nw_ref[...]).astype(BF16)
    acc_ref[...] = x
    n_cols = wg_ref.shape[1]
    for c0 in range(0, n_cols, FFN_COLS):
        g = _dot(h, wg_ref[:, c0:c0 + FFN_COLS])
        u = _dot(h, wu_ref[:, c0:c0 + FFN_COLS])
        a = (_silu(g) * u).astype(BF16)
        acc_ref[...] += _dot(a, wd_ref[c0:c0 + FFN_COLS, :])
    o_ref[...] = _rms(acc_ref[...], nf_ref[...])


def _ffn(x2d, norm_w, wg, wu, wd, norm_final):
    t, d = x2d.shape
    tm = min(FFN_ROWS, t)
    row = pl.BlockSpec((tm, d), lambda i: (i, 0))
    return pl.pallas_call(
        _ffn_kernel,
        grid=(t // tm,),
        in_specs=[row, _const_spec(norm_w.shape), _const_spec(wg.shape), _const_spec(wu.shape),
                  _const_spec(wd.shape), _const_spec(norm_final.shape)],
        out_specs=row,
        out_shape=jax.ShapeDtypeStruct((t, d), F32),
        scratch_shapes=[pltpu.VMEM((tm, d), F32)],
        compiler_params=_cparams(("arbitrary",)),
        name="ffn",
    )(x2d, norm_w, wg, wu, wd, norm_final)


def _pad_lanes(v):
    return jnp.pad(v.astype(F32), (0, LANES - v.shape[0])).reshape(1, LANES)


def kernel(x, mem, norm_mix_w, w_in, conv_w, conv_b, dt_bias, a_log, d_skip, ssd_norm_w,
           hg_lower_bounds, hg_norm_w, w_out, norm_xa_w, norm_mem_w, xa_wq, xa_wkv, xa_wo,
           norm_ffn_w, ffn_w_gate, ffn_w_up, ffn_w_down, norm_final_w):
    bsz, seqlen, d = x.shape
    assert d == D_MODEL and norm_mix_w.shape[0] == 1, "single-layer block of width 1024"
    assert hg_lower_bounds.shape[0] == 2
    assert seqlen % SSD_CHUNK == 0 and seqlen % HG_CHUNK == 0
    t = bsz * seqlen
    x2d = x.reshape(t, d)
    row = lambda v: v.reshape(1, -1).astype(F32)

    s1 = SSD_DIM
    s2 = s1 + SSD_CONV_CH
    s3 = s2 + SSD_HEADS
    w = w_in[0]
    w_main = jnp.concatenate([w[:, :s2], w[:, s3:]], axis=1).astype(BF16)
    w_dt = jnp.pad(w[:, s2:s3], ((0, 0), (0, LANES - SSD_HEADS))).astype(BF16)

    z, xbc, dt, hq, hf, hi, hgate = _in_proj(x2d, row(norm_mix_w[0]), w_main, w_dt)

    expand = (np.arange(LANES)[:, None] == (np.arange(SSD_DIM)[None, :] // SSD_HEAD_DIM))
    expand = jnp.asarray(expand, BF16)
    d_skip_w = jnp.repeat(d_skip[0].astype(F32), SSD_HEAD_DIM).reshape(1, SSD_DIM)
    y_a = _ssd(z, xbc, dt, conv_w[0].astype(F32), row(conv_b[0]), _pad_lanes(dt_bias[0]),
               _pad_lanes(a_log[0]), d_skip_w, row(ssd_norm_w[0]), expand,
               seqlen // SSD_CHUNK)
    o_b = _hgrn2(hq, hf, hi, hgate, hg_lower_bounds.astype(F32), row(hg_norm_w[0]),
                 seqlen // HG_CHUNK)

    km, vm = _mem_kv(mem, row(norm_mem_w[0]), xa_wkv[0].astype(BF16))
    x2 = _attn(x2d, y_a, o_b, w_out[0].astype(BF16), row(norm_xa_w[0]), xa_wq[0].astype(BF16),
               km, vm, xa_wo[0].astype(BF16), seqlen)
    out = _ffn(x2, row(norm_ffn_w[0]), ffn_w_gate[0].astype(BF16), ffn_w_up[0].astype(BF16),
               ffn_w_down[0].astype(BF16), row(norm_final_w))
    return out.reshape(bsz, seqlen, d)
```

```python
import functools
import math

import numpy as np
import jax
import jax.numpy as jnp
from jax import lax
from jax.experimental import pallas as pl
from jax.experimental.pallas import tpu as pltpu

F32 = jnp.float32
BF16 = jnp.bfloat16
EPS = 1e-6
LOG2E = math.log2(math.e)

D_MODEL = 1024
SSD_HEADS = 16
SSD_HEAD_DIM = 64
SSD_DIM = SSD_HEADS * SSD_HEAD_DIM
SSD_GROUPS = 2
SSD_STATE = 128
SSD_CONV = 4
SSD_CHUNK = 128
SSD_CONV_CH = SSD_DIM + 2 * SSD_GROUPS * SSD_STATE
HG_HEADS = 8
HG_K = 128
HG_V = 128
HG_KDIM = HG_HEADS * HG_K
HG_VDIM = HG_HEADS * HG_V
XA_HEADS = 4
XA_HEAD_DIM = D_MODEL // XA_HEADS

LANES = 128
SUBLANES = 8
CONV_TAIL = 16
VMEM_LIMIT_BYTES = 56 * 1024 * 1024

HG_CHUNK = 128
ATTN_ROWS = 512
FFN_ROWS = 512
FFN_COLS = 256

SSD_PROJ = SSD_DIM + SSD_CONV_CH + LANES
HG_PROJ = 2 * HG_KDIM + 2 * HG_VDIM


def _cparams(sem):
    return pltpu.CompilerParams(dimension_semantics=sem, vmem_limit_bytes=VMEM_LIMIT_BYTES)


def _const_spec(shape):
    zeros = (0,) * len(shape)
    return pl.BlockSpec(shape, lambda *_: zeros)


def _dot(a, b):
    return jnp.dot(a, b, preferred_element_type=F32)


def _dot_nt(a, b):
    return lax.dot_general(a, b, (((1,), (1,)), ((), ())), preferred_element_type=F32)


def _dot_tn(a, b):
    return lax.dot_general(a, b, (((0,), (0,)), ((), ())), preferred_element_type=F32)


def _sigmoid(x):
    return 1.0 / (1.0 + jnp.exp(-x))


def _silu(x):
    return x * _sigmoid(x)


def _rms(x, w):
    ms = jnp.mean(x * x, axis=-1, keepdims=True)
    return x * lax.rsqrt(ms + EPS) * w


def _split_hi_lo(x):
    hi = x.astype(BF16)
    lo = (x - hi.astype(F32)).astype(BF16)
    return hi, lo


def _chunk_starts_sequence(step, chunks_per_batch):
    return jnp.logical_or(step == 0, (step + chunks_per_batch - 1) % chunks_per_batch == 0)


def _mem_kv_kernel(mem_ref, nw_ref, wkv_ref, k_ref, v_ref):
    m = _rms(mem_ref[0], nw_ref[...]).astype(BF16)
    kv = _dot(m, wkv_ref[...])
    k_ref[0] = kv[:, :D_MODEL].astype(BF16)
    v_ref[0] = kv[:, D_MODEL:].astype(BF16)


def _mem_kv(mem, norm_w, wkv):
    bsz, mlen, d = mem.shape
    out = jax.ShapeDtypeStruct((bsz, mlen, d), BF16)
    return pl.pallas_call(
        _mem_kv_kernel,
        grid=(bsz,),
        in_specs=[pl.BlockSpec((1, mlen, d), lambda b: (b, 0, 0)),
                  _const_spec((1, d)), _const_spec((d, 2 * d))],
        out_specs=[pl.BlockSpec((1, mlen, d), lambda b: (b, 0, 0))] * 2,
        out_shape=[out, out],
        compiler_params=_cparams(("arbitrary",)),
        name="mem_kv",
    )(mem, norm_w, wkv)


def _ssd_kernel(x_ref, nmw_ref, w_ref, convw_ref, convb_ref, dtb_ref, alog_ref,
                dskip_ref, nw_ref, expand_ref, shift_ref, o_ref,
                proj_ref, ext_ref, state_ref, y_ref, zs_ref, *, chunks_per_batch):
    q = SSD_CHUNK
    gw = SSD_DIM // SSD_GROUPS
    step = pl.program_id(0)
    first = _chunk_starts_sequence(step, chunks_per_batch)

    @pl.when(step == 0)
    def _():
        proj_ref[...] = jnp.zeros((q, SSD_PROJ), F32)

    @pl.when(first)
    def _():
        ext_ref[0:CONV_TAIL, :] = jnp.zeros((CONV_TAIL, SSD_CONV_CH), BF16)
        state_ref[...] = jnp.zeros_like(state_ref)

    @pl.when(jnp.logical_not(first))
    def _():
        ext_ref[0:CONV_TAIL, :] = ext_ref[q:q + CONV_TAIL, :]

    zs_ref[...] = _silu(proj_ref[:, 0:SSD_DIM])
    u_cur = proj_ref[:, SSD_DIM:SSD_DIM + SSD_CONV_CH].astype(BF16)
    ext_ref[CONV_TAIL:CONV_TAIL + q, :] = u_cur
    dtr = proj_ref[:, SSD_DIM + SSD_CONV_CH:] + dtb_ref[...]

    shifted = _dot(shift_ref[...], ext_ref[...])
    acc = convb_ref[...] + convw_ref[SSD_CONV - 1:SSD_CONV, :] * u_cur.astype(F32)
    for k in range(SSD_CONV - 1):
        acc = acc + convw_ref[k:k + 1, :] * shifted[k * q:(k + 1) * q]
    xbc = _silu(acc)
    xs = xbc[:, :SSD_DIM]
    b_all = xbc[:, SSD_DIM:SSD_DIM + SSD_GROUPS * SSD_STATE]
    c_all = xbc[:, SSD_DIM + SSD_GROUPS * SSD_STATE:]

    dt = jnp.maximum(dtr, 0.0) + jnp.log(1.0 + jnp.exp(-jnp.abs(dtr)))
    da = dt * (-jnp.exp(alog_ref[...]) * LOG2E)
    row = lax.broadcasted_iota(jnp.int32, (q, q), 0)
    colm = lax.broadcasted_iota(jnp.int32, (q, q), 1)
    causal = row >= colm
    tri = causal.astype(BF16)
    da_hi, da_lo = _split_hi_lo(da)
    da_lo2 = (da - da_hi.astype(F32) - da_lo.astype(F32)).astype(BF16)
    acum = _dot(tri, da_hi) + _dot(tri, da_lo) + _dot(tri, da_lo2)
    acum_t = acum.T
    a_last = acum[q - 1:q, :]
    dec_from_start = jnp.exp2(acum)
    dec_to_end = jnp.exp2(a_last - acum)
    chunk_decay = jnp.broadcast_to(jnp.exp2(a_last), (SUBLANES, LANES))

    stack = jnp.concatenate([dt, dec_from_start, dec_to_end, chunk_decay], axis=0)
    s_hi, s_lo = _split_hi_lo(stack)
    wide = _dot(s_hi, expand_ref[...]) + _dot(s_lo, expand_ref[...])
    dt_w = wide[0:q]
    dfs_w = wide[q:2 * q]
    dte_w = wide[2 * q:3 * q]
    cd_w = wide[3 * q:3 * q + 1]

    xdt = xs * dt_w
    xw = (xdt * dte_w).astype(BF16)

    lane = lax.broadcasted_iota(jnp.int32, (q, LANES), 1)
    left = lane < SSD_HEAD_DIM
    heads_per_group = SSD_HEADS // SSD_GROUPS
    for g in range(SSD_GROUPS):
        b_g = b_all[:, g * SSD_STATE:(g + 1) * SSD_STATE].astype(BF16)
        c_g = c_all[:, g * SSD_STATE:(g + 1) * SSD_STATE].astype(BF16)
        cb = _dot_nt(c_g, b_g)
        gsl = slice(g * gw, (g + 1) * gw)
        s_in = state_ref[g]
        y_ref[:, gsl] = _dot(c_g, s_in.astype(BF16)) * dfs_w[:, gsl]
        state_ref[g] = s_in * cd_w[:, gsl] + _dot_tn(b_g, xw[:, gsl])
        for pair in range(heads_per_group // 2):
            h0 = g * heads_per_group + 2 * pair
            ms = []
            for h in (h0, h0 + 1):
                seg = acum[:, h:h + 1] - acum_t[h:h + 1, :]
                l_dec = jnp.exp2(jnp.where(causal, seg, -jnp.inf))
                ms.append((cb * l_dec).astype(BF16))
            lhs = jnp.concatenate(ms, axis=1)
            psl = slice(h0 * SSD_HEAD_DIM, (h0 + 2) * SSD_HEAD_DIM)
            xp = xdt[:, psl]
            rhs = jnp.concatenate([jnp.where(left, xp, 0.0), jnp.where(left, 0.0, xp)],
                                  axis=0).astype(BF16)
            y_ref[:, psl] += _dot(lhs, rhs)

    y = y_ref[...] + dskip_ref[...] * xs
    yz = y * zs_ref[...]
    for g in range(SSD_GROUPS):
        gsl = slice(g * gw, (g + 1) * gw)
        o_ref[:, gsl] = _rms(yz[:, gsl], nw_ref[:, gsl]).astype(o_ref.dtype)

    h_next = _rms(x_ref[...], nmw_ref[...]).astype(BF16)
    proj_ref[...] = _dot(h_next, w_ref[...])


def _conv_shift_matrix(q):
    s = np.zeros(((SSD_CONV - 1) * q, CONV_TAIL + q), np.float32)
    for k in range(SSD_CONV - 1):
        for t in range(q):
            s[k * q + t, CONV_TAIL + t - (SSD_CONV - 1) + k] = 1.0
    return jnp.asarray(s, BF16)


def _lookahead_specs(n_chunks, rows, width_in, width_out):
    x_spec = pl.BlockSpec((rows, width_in), lambda s: (jnp.minimum(s, n_chunks - 1), 0))
    o_spec = pl.BlockSpec((rows, width_out), lambda s: (jnp.maximum(s - 1, 0), 0))
    return x_spec, o_spec


def _ssd(x2d, norm_mix_w, w_ssd, conv_w, conv_b, dt_bias, a_log, d_skip_w, norm_w, expand,
         chunks_per_batch):
    t, d = x2d.shape
    q = SSD_CHUNK
    n_chunks = t // q
    shift = _conv_shift_matrix(q)
    kern = functools.partial(_ssd_kernel, chunks_per_batch=chunks_per_batch)
    x_spec, o_spec = _lookahead_specs(n_chunks, q, d, SSD_DIM)
    consts = (norm_mix_w, w_ssd, conv_w, conv_b, dt_bias, a_log, d_skip_w, norm_w, expand, shift)
    return pl.pallas_call(
        kern,
        grid=(n_chunks + 1,),
        in_specs=[x_spec] + [_const_spec(c.shape) for c in consts],
        out_specs=o_spec,
        out_shape=jax.ShapeDtypeStruct((t, SSD_DIM), BF16),
        scratch_shapes=[pltpu.VMEM((q, SSD_PROJ), F32),
                        pltpu.VMEM((CONV_TAIL + q, SSD_CONV_CH), BF16),
                        pltpu.VMEM((SSD_GROUPS, SSD_STATE, SSD_DIM // SSD_GROUPS), F32),
                        pltpu.VMEM((q, SSD_DIM), F32),
                        pltpu.VMEM((q, SSD_DIM), F32)],
        compiler_params=_cparams(("arbitrary",)),
        name="ssd",
    )(x2d, *consts)


HG_PROJ_PIECES = 8
HG_VPU_MIN_HALF = 8


def _hgrn2_levels(c):
    levels = []
    s = c // 2
    while s >= 1:
        levels.append(s)
        s //= 2
    return levels


def _hgrn2_tables(c):
    idx = np.arange(c)
    blocks = [np.tril(np.ones((c, c), np.float32))]
    masks = []
    for s in _hgrn2_levels(c):
        mid = (idx // (2 * s)) * (2 * s) + s
        upper = idx >= mid
        if s < HG_VPU_MIN_HALF:
            a = np.zeros((c, c), np.float32)
            for i in range(c):
                if upper[i]:
                    a[i, mid[i]:i + 1] = 1.0
                else:
                    a[i, i + 1:mid[i]] = 1.0
            blocks.append(a)
        same = (idx[:, None] // (2 * s)) == (idx[None, :] // (2 * s))
        masks.append((same & upper[:, None] & (~upper)[None, :]).astype(np.float32))
    masks.append(np.eye(c, dtype=np.float32))
    sums = np.concatenate(blocks, axis=0)
    sums2 = np.concatenate([sums, sums], axis=1)
    masks = np.stack(masks)
    masks2 = np.concatenate([masks, masks], axis=2)
    return jnp.asarray(sums2, BF16), jnp.asarray(masks2, F32)


def _hgrn2_kernel(x_ref, nmw_ref, w_ref, lbp_ref, nw_ref, sums_ref, mask_ref, o_ref,
                  proj_ref, state_ref, g2_ref, q_ref, k_ref, b_ref, ql_ref, klt_ref, qe_ref,
                  ke_ref, v_ref, gate_ref, *, chunks_per_batch):
    c = HG_CHUNK
    levels = _hgrn2_levels(c)
    n_levels = len(levels)
    step = pl.program_id(0)

    @pl.when(step == 0)
    def _():
        proj_ref[...] = jnp.zeros((c, HG_PROJ), F32)

    @pl.when(_chunk_starts_sequence(step, chunks_per_batch))
    def _():
        state_ref[...] = jnp.zeros_like(state_ref)

    lbp = lbp_ref[...]
    e = jnp.exp(lbp - jnp.max(lbp, axis=0, keepdims=True))
    lb = e[0:1, :] / jnp.sum(e, axis=0, keepdims=True)

    q_ref[...] = _silu(proj_ref[:, 0:HG_KDIM])
    fg = lb + (1.0 - lb) * _sigmoid(proj_ref[:, HG_KDIM:2 * HG_KDIM])
    v_ref[...] = proj_ref[:, 2 * HG_KDIM:2 * HG_KDIM + HG_VDIM].astype(BF16)
    gate_ref[...] = _silu(proj_ref[:, 2 * HG_KDIM + HG_VDIM:])
    g_hi, g_lo = _split_hi_lo(jnp.log2(fg))
    g2_ref[0:c, :] = g_hi
    g2_ref[c:2 * c, :] = g_lo
    k_ref[...] = 1.0 - fg

    h_next = _rms(x_ref[...], nmw_ref[...]).astype(BF16)
    piece = HG_PROJ // HG_PROJ_PIECES

    def project(i):
        cols = slice(i * piece, (i + 1) * piece)
        proj_ref[:, cols] = _dot(h_next, w_ref[:, cols])

    project(0)
    project(1)
    fine = _dot(sums_ref[...], g2_ref[...])
    b_ref[...] = fine[0:c]
    b_last = b_ref[c - 1:c, :]
    qe_ref[...] = (q_ref[...] * jnp.exp2(b_ref[...])).astype(BF16)
    ke_ref[...] = (k_ref[...] * jnp.exp2(b_last - b_ref[...])).astype(BF16)
    state_decay = jnp.exp2(b_last)

    n_fine = 0
    for lvl, s in enumerate(levels):
        if s >= HG_VPU_MIN_HALF:
            q_rows, k_rows = [], []
            for r0 in range(0, c, 2 * s):
                lo = slice(r0, r0 + s)
                up = slice(r0 + s, r0 + 2 * s)
                ref = b_ref[r0 + s - 1:r0 + s, :]
                zeros = jnp.zeros((s, HG_KDIM), F32)
                q_rows += [zeros, q_ref[up, :] * jnp.exp2(b_ref[up, :] - ref)]
                k_rows += [k_ref[lo, :] * jnp.exp2(ref - b_ref[lo, :]), zeros]
            ql = jnp.concatenate(q_rows, axis=0)
            kl = jnp.concatenate(k_rows, axis=0)
        else:
            n_fine += 1
            ex = jnp.exp2(fine[n_fine * c:(n_fine + 1) * c])
            ql = q_ref[...] * ex
            kl = k_ref[...] * ex
        ql_ref[lvl] = ql.astype(BF16)
        klt_ref[lvl] = kl.T
        if lvl < HG_PROJ_PIECES - 2:
            project(2 + lvl)
    ql_ref[n_levels] = q_ref[...].astype(BF16)
    klt_ref[n_levels] = k_ref[...].T

    zero = jnp.zeros((HG_K, c), BF16)
    for pair in range(HG_HEADS // 2):
        psl = slice(2 * pair * HG_K, (2 * pair + 2) * HG_K)
        att = None
        for lvl in range(n_levels + 1):
            kt = klt_ref[lvl, psl, :].astype(BF16)
            kbd = jnp.concatenate(
                [jnp.concatenate([kt[:HG_K], zero], axis=1),
                 jnp.concatenate([zero, kt[HG_K:]], axis=1)], axis=0)
            p = _dot(ql_ref[lvl, :, psl], kbd)
            att = p if lvl == 0 else att + p * mask_ref[lvl]
        for sub in range(2):
            h = 2 * pair + sub
            sl = slice(h * HG_K, (h + 1) * HG_K)
            vh = v_ref[:, sl]
            s_t = state_ref[h]
            o = _dot_nt(qe_ref[:, sl], s_t.astype(BF16))
            o = o + _dot(att[:, sub * c:(sub + 1) * c].astype(BF16), vh)
            state_ref[h] = s_t * state_decay[:, sl] + _dot_tn(vh, ke_ref[:, sl])
            o_ref[:, sl] = (_rms(o, nw_ref[...]) * gate_ref[:, sl]).astype(o_ref.dtype)


def _hgrn2(x2d, norm_mix_w, w_hg, lb_params, norm_w, chunks_per_batch):
    t, d = x2d.shape
    c = HG_CHUNK
    n_chunks = t // c
    n_levels = len(_hgrn2_levels(c))
    sums, masks = _hgrn2_tables(c)
    kern = functools.partial(_hgrn2_kernel, chunks_per_batch=chunks_per_batch)
    x_spec, o_spec = _lookahead_specs(n_chunks, c, d, HG_VDIM)
    consts = (norm_mix_w, w_hg, lb_params, norm_w, sums, masks)
    return pl.pallas_call(
        kern,
        grid=(n_chunks + 1,),
        in_specs=[x_spec] + [_const_spec(a.shape) for a in consts],
        out_specs=o_spec,
        out_shape=jax.ShapeDtypeStruct((t, HG_VDIM), BF16),
        scratch_shapes=[pltpu.VMEM((c, HG_PROJ), F32),
                        pltpu.VMEM((HG_HEADS, HG_V, HG_K), F32),
                        pltpu.VMEM((2 * c, HG_KDIM), BF16),
                        pltpu.VMEM((c, HG_KDIM), F32),
                        pltpu.VMEM((c, HG_KDIM), F32),
                        pltpu.VMEM((c, HG_KDIM), F32),
                        pltpu.VMEM((n_levels + 1, c, HG_KDIM), BF16),
                        pltpu.VMEM((n_levels + 1, HG_KDIM, c), F32),
                        pltpu.VMEM((c, HG_KDIM), BF16),
                        pltpu.VMEM((c, HG_KDIM), BF16),
                        pltpu.VMEM((c, HG_VDIM), BF16),
                        pltpu.VMEM((c, HG_VDIM), F32)],
        compiler_params=_cparams(("arbitrary",)),
        name="hgrn2",
    )(x2d, *consts)


def _attn_kernel(x_ref, ya_ref, ob_ref, wout_ref, nw_ref, wq_ref, km_ref, vm_ref, wo_ref,
                 o_ref, ox_ref):
    x1 = x_ref[...] + _dot(ya_ref[...], wout_ref[0:SSD_DIM, :]) \
        + _dot(ob_ref[...], wout_ref[SSD_DIM:, :])
    h = _rms(x1, nw_ref[...]).astype(BF16)
    qx = (_dot(h, wq_ref[...]) * (XA_HEAD_DIM ** -0.5)).astype(BF16)
    for hd in range(XA_HEADS):
        sl = slice(hd * XA_HEAD_DIM, (hd + 1) * XA_HEAD_DIM)
        sc = _dot_nt(qx[:, sl], km_ref[0, :, sl])
        p = jnp.exp(sc - jnp.max(sc, axis=-1, keepdims=True))
        denom = jnp.sum(p, axis=-1, keepdims=True)
        ox_ref[:, sl] = (_dot(p.astype(BF16), vm_ref[0, :, sl]) / denom).astype(BF16)
    o_ref[...] = x1 + _dot(ox_ref[...], wo_ref[...])


def _attn(x2d, ya, ob, w_out, norm_w, wq, km, vm, wo, seqlen):
    t, d = x2d.shape
    tm = min(ATTN_ROWS, seqlen)
    tpb = seqlen // tm
    mlen = km.shape[1]
    row = lambda n: pl.BlockSpec((tm, n), lambda i: (i, 0))
    mem = pl.BlockSpec((1, mlen, d), lambda i: (i // tpb, 0, 0))
    return pl.pallas_call(
        _attn_kernel,
        grid=(t // tm,),
        in_specs=[row(d), row(SSD_DIM), row(HG_VDIM), _const_spec(w_out.shape),
                  _const_spec(norm_w.shape), _const_spec(wq.shape), mem, mem,
                  _const_spec(wo.shape)],
        out_specs=row(d),
        out_shape=jax.ShapeDtypeStruct((t, d), F32),
        scratch_shapes=[pltpu.VMEM((tm, d), BF16)],
        compiler_params=_cparams(("arbitrary",)),
        name="attn",
    )(x2d, ya, ob, w_out, norm_w, wq, km, vm, wo)


def _ffn_kernel(x_ref, nw_ref, wg_ref, wu_ref, wd_ref, nf_ref, o_ref, acc_ref):
    x = x_ref[...]
    h = _rms(x, nw_ref[...]).astype(BF16)
    acc_ref[...] = x
    n_cols = wg_ref.shape[1]
    for c0 in range(0, n_cols, FFN_COLS):
        g = _dot(h, wg_ref[:, c0:c0 + FFN_COLS])
        u = _dot(h, wu_ref[:, c0:c0 + FFN_COLS])
        a = (_silu(g) * u).astype(BF16)
        acc_ref[...] += _dot(a, wd_ref[c0:c0 + FFN_COLS, :])
    o_ref[...] = _rms(acc_ref[...], nf_ref[...])


def _ffn(x2d, norm_w, wg, wu, wd, norm_final):
    t, d = x2d.shape
    tm = min(FFN_ROWS, t)
    row = pl.BlockSpec((tm, d), lambda i: (i, 0))
    return pl.pallas_call(
        _ffn_kernel,
        grid=(t // tm,),
        in_specs=[row, _const_spec(norm_w.shape), _const_spec(wg.shape), _const_spec(wu.shape),
                  _const_spec(wd.shape), _const_spec(norm_final.shape)],
        out_specs=row,
        out_shape=jax.ShapeDtypeStruct((t, d), F32),
        scratch_shapes=[pltpu.VMEM((tm, d), F32)],
        compiler_params=_cparams(("arbitrary",)),
        name="ffn",
    )(x2d, norm_w, wg, wu, wd, norm_final)


def _pad_lanes(v):
    return jnp.pad(v.astype(F32), (0, LANES - v.shape[0])).reshape(1, LANES)


def kernel(x, mem, norm_mix_w, w_in, conv_w, conv_b, dt_bias, a_log, d_skip, ssd_norm_w,
           hg_lower_bounds, hg_norm_w, w_out, norm_xa_w, norm_mem_w, xa_wq, xa_wkv, xa_wo,
           norm_ffn_w, ffn_w_gate, ffn_w_up, ffn_w_down, norm_final_w):
    bsz, seqlen, d = x.shape
    assert d == D_MODEL and norm_mix_w.shape[0] == 1, "single-layer block of width 1024"
    assert hg_lower_bounds.shape[0] == 2
    assert seqlen % SSD_CHUNK == 0 and seqlen % HG_CHUNK == 0
    t = bsz * seqlen
    x2d = x.reshape(t, d)
    row = lambda v: v.reshape(1, -1).astype(F32)

    s3 = SSD_DIM + SSD_CONV_CH + SSD_HEADS
    w = w_in[0]
    w_ssd = jnp.pad(w[:, :s3], ((0, 0), (0, LANES - SSD_HEADS))).astype(BF16)
    w_hg = w[:, s3:].astype(BF16)

    expand = (np.arange(LANES)[:, None] == (np.arange(SSD_DIM)[None, :] // SSD_HEAD_DIM))
    expand = jnp.asarray(expand, BF16)
    d_skip_w = jnp.repeat(d_skip[0].astype(F32), SSD_HEAD_DIM).reshape(1, SSD_DIM)
    y_a = _ssd(x2d, row(norm_mix_w[0]), w_ssd, conv_w[0].astype(F32), row(conv_b[0]),
               _pad_lanes(dt_bias[0]), _pad_lanes(a_log[0]), d_skip_w, row(ssd_norm_w[0]),
               expand, seqlen // SSD_CHUNK)
    o_b = _hgrn2(x2d, row(norm_mix_w[0]), w_hg, hg_lower_bounds.astype(F32), row(hg_norm_w[0]),
                 seqlen // HG_CHUNK)

    km, vm = _mem_kv(mem, row(norm_mem_w[0]), xa_wkv[0].astype(BF16))
    x2 = _attn(x2d, y_a, o_b, w_out[0].astype(BF16), row(norm_xa_w[0]), xa_wq[0].astype(BF16),
               km, vm, xa_wo[0].astype(BF16), seqlen)
    out = _ffn(x2, row(norm_ffn_w[0]), ffn_w_gate[0].astype(BF16), ffn_w_up[0].astype(BF16),
               ffn_w_down[0].astype(BF16), row(norm_final_w))
    return out.reshape(bsz, seqlen, d)
```

```python
import math

import numpy as np
import jax
import jax.numpy as jnp
from jax import lax
from jax.experimental import pallas as pl
from jax.experimental.pallas import tpu as pltpu

F32 = jnp.float32
BF16 = jnp.bfloat16
EPS = 1e-6
LOG2E = math.log2(math.e)

D_MODEL = 1024
SSD_HEADS = 16
SSD_HEAD_DIM = 64
SSD_DIM = SSD_HEADS * SSD_HEAD_DIM
SSD_GROUPS = 2
SSD_STATE = 128
SSD_CONV = 4
SSD_CHUNK = 128
SSD_CONV_CH = SSD_DIM + 2 * SSD_GROUPS * SSD_STATE
HG_HEADS = 8
HG_K = 128
HG_V = 128
HG_KDIM = HG_HEADS * HG_K
HG_VDIM = HG_HEADS * HG_V
XA_HEADS = 4
XA_HEAD_DIM = D_MODEL // XA_HEADS

LANES = 128
SUBLANES = 8
CONV_TAIL = 16
VMEM_LIMIT_BYTES = 56 * 1024 * 1024

HG_CHUNK = 128
ATTN_ROWS = 512
FFN_ROWS = 512
FFN_COLS = 256

SSD_PROJ = SSD_DIM + SSD_CONV_CH + LANES
SSD_PROJ_PIECE = 512
HG_PROJ = 2 * HG_KDIM + 2 * HG_VDIM


def _cparams(sem):
    return pltpu.CompilerParams(dimension_semantics=sem, vmem_limit_bytes=VMEM_LIMIT_BYTES)


def _const_spec(shape):
    zeros = (0,) * len(shape)
    return pl.BlockSpec(shape, lambda *_: zeros)


def _dot(a, b):
    return jnp.dot(a, b, preferred_element_type=F32)


def _dot_nt(a, b):
    return lax.dot_general(a, b, (((1,), (1,)), ((), ())), preferred_element_type=F32)


def _dot_tn(a, b):
    return lax.dot_general(a, b, (((0,), (0,)), ((), ())), preferred_element_type=F32)


def _sigmoid(x):
    return 1.0 / (1.0 + jnp.exp(-x))


def _silu(x):
    return x * _sigmoid(x)


def _rms(x, w):
    ms = jnp.mean(x * x, axis=-1, keepdims=True)
    return x * lax.rsqrt(ms + EPS) * w


def _split_hi_lo(x):
    hi = x.astype(BF16)
    lo = (x - hi.astype(F32)).astype(BF16)
    return hi, lo


def _mem_kv_kernel(mem_ref, nw_ref, wkv_ref, k_ref, v_ref):
    m = _rms(mem_ref[0], nw_ref[...]).astype(BF16)
    kv = _dot(m, wkv_ref[...])
    k_ref[0] = kv[:, :D_MODEL].astype(BF16)
    v_ref[0] = kv[:, D_MODEL:].astype(BF16)


def _mem_kv(mem, norm_w, wkv):
    bsz, mlen, d = mem.shape
    out = jax.ShapeDtypeStruct((bsz, mlen, d), BF16)
    return pl.pallas_call(
        _mem_kv_kernel,
        grid=(bsz,),
        in_specs=[pl.BlockSpec((1, mlen, d), lambda b: (b, 0, 0)),
                  _const_spec((1, d)), _const_spec((d, 2 * d))],
        out_specs=[pl.BlockSpec((1, mlen, d), lambda b: (b, 0, 0))] * 2,
        out_shape=[out, out],
        compiler_params=_cparams(("arbitrary",)),
        name="mem_kv",
    )(mem, norm_w, wkv)


def _ssd_kernel(x_ref, nmw_ref, w_ref, convw_ref, convb_ref, dtb_ref, alog_ref,
                dskip_ref, nw_ref, expand_ref, shift_ref, o_ref,
                proj_ref, ext_ref, state_ref, y_ref, zs_ref):
    n_seq, q = x_ref.shape[0], x_ref.shape[1]
    gw = SSD_DIM // SSD_GROUPS
    heads_per_group = SSD_HEADS // SSD_GROUPS
    step = pl.program_id(0)
    seqs = range(n_seq)

    @pl.when(step == 0)
    def _():
        proj_ref[...] = jnp.zeros_like(proj_ref)

    @pl.when(step <= 1)
    def _():
        ext_ref[:, 0:CONV_TAIL, :] = jnp.zeros((n_seq, CONV_TAIL, SSD_CONV_CH), BF16)
        state_ref[...] = jnp.zeros_like(state_ref)

    @pl.when(step > 1)
    def _():
        ext_ref[:, 0:CONV_TAIL, :] = ext_ref[:, q:q + CONV_TAIL, :]

    row = lax.broadcasted_iota(jnp.int32, (q, q), 0)
    colm = lax.broadcasted_iota(jnp.int32, (q, q), 1)
    causal = row >= colm
    tri = causal.astype(BF16)
    lane = lax.broadcasted_iota(jnp.int32, (q, LANES), 1)
    left = lane < SSD_HEAD_DIM

    h_next = _rms(x_ref[...].reshape(n_seq * q, D_MODEL), nmw_ref[...]).astype(BF16)
    n_pieces = -(-SSD_PROJ // SSD_PROJ_PIECE)

    def project(j):
        cols = slice(j * SSD_PROJ_PIECE, min((j + 1) * SSD_PROJ_PIECE, SSD_PROJ))
        proj_ref[:, cols] = _dot(h_next, w_ref[:, cols])

    xs, xdt, xw, b_all, c_all, acum, acum_t, dfs_w, cd_w = ([None] * n_seq for _ in range(9))
    u_cur, dtr = [None] * n_seq, [None] * n_seq
    for i in seqs:
        rows = slice(i * q, (i + 1) * q)
        zs_ref[i] = _silu(proj_ref[rows, 0:SSD_DIM])
        u_cur[i] = proj_ref[rows, SSD_DIM:SSD_DIM + SSD_CONV_CH].astype(BF16)
        ext_ref[i, CONV_TAIL:CONV_TAIL + q, :] = u_cur[i]
        dtr[i] = proj_ref[rows, SSD_DIM + SSD_CONV_CH:] + dtb_ref[...]

    for i in seqs:
        shifted = _dot(shift_ref[...], ext_ref[i])
        project(2 * i)
        acc = convb_ref[...] + convw_ref[SSD_CONV - 1:SSD_CONV, :] * u_cur[i].astype(F32)
        for k in range(SSD_CONV - 1):
            acc = acc + convw_ref[k:k + 1, :] * shifted[k * q:(k + 1) * q]
        xbc = _silu(acc)
        xs[i] = xbc[:, :SSD_DIM]
        b_all[i] = xbc[:, SSD_DIM:SSD_DIM + SSD_GROUPS * SSD_STATE]
        c_all[i] = xbc[:, SSD_DIM + SSD_GROUPS * SSD_STATE:]

        dt = jnp.maximum(dtr[i], 0.0) + jnp.log(1.0 + jnp.exp(-jnp.abs(dtr[i])))
        da = dt * (-jnp.exp(alog_ref[...]) * LOG2E)
        da_hi, da_lo = _split_hi_lo(da)
        da_lo2 = (da - da_hi.astype(F32) - da_lo.astype(F32)).astype(BF16)
        acum[i] = _dot(tri, da_hi) + _dot(tri, da_lo) + _dot(tri, da_lo2)
        project(2 * i + 1)
        acum_t[i] = acum[i].T
        a_last = acum[i][q - 1:q, :]
        dec_from_start = jnp.exp2(acum[i])
        dec_to_end = jnp.exp2(a_last - acum[i])
        chunk_decay = jnp.broadcast_to(jnp.exp2(a_last), (SUBLANES, LANES))

        stack = jnp.concatenate([dt, dec_from_start, dec_to_end, chunk_decay], axis=0)
        s_hi, s_lo = _split_hi_lo(stack)
        wide = _dot(s_hi, expand_ref[...]) + _dot(s_lo, expand_ref[...])
        dfs_w[i] = wide[q:2 * q]
        cd_w[i] = wide[3 * q:3 * q + 1]
        xdt[i] = xs[i] * wide[0:q]
        xw[i] = (xdt[i] * wide[2 * q:3 * q]).astype(BF16)

    for g in range(SSD_GROUPS):
        gsl = slice(g * gw, (g + 1) * gw)
        for i in seqs:
            b_g = b_all[i][:, g * SSD_STATE:(g + 1) * SSD_STATE].astype(BF16)
            c_g = c_all[i][:, g * SSD_STATE:(g + 1) * SSD_STATE].astype(BF16)
            cb = _dot_nt(c_g, b_g)
            s_in = state_ref[i, g]
            y_ref[i, :, gsl] = _dot(c_g, s_in.astype(BF16)) * dfs_w[i][:, gsl]
            state_ref[i, g] = s_in * cd_w[i][:, gsl] + _dot_tn(b_g, xw[i][:, gsl])
            for pair in range(heads_per_group // 2):
                h0 = g * heads_per_group + 2 * pair
                ms = []
                for h in (h0, h0 + 1):
                    seg = acum[i][:, h:h + 1] - acum_t[i][h:h + 1, :]
                    l_dec = jnp.exp2(jnp.where(causal, seg, -jnp.inf))
                    ms.append((cb * l_dec).astype(BF16))
                lhs = jnp.concatenate(ms, axis=1)
                psl = slice(h0 * SSD_HEAD_DIM, (h0 + 2) * SSD_HEAD_DIM)
                xp = xdt[i][:, psl]
                rhs = jnp.concatenate([jnp.where(left, xp, 0.0), jnp.where(left, 0.0, xp)],
                                      axis=0).astype(BF16)
                y_ref[i, :, psl] += _dot(lhs, rhs)

    for j in range(2 * n_seq, n_pieces):
        project(j)
    for i in seqs:
        yz = (y_ref[i] + dskip_ref[...] * xs[i]) * zs_ref[i]
        for g in range(SSD_GROUPS):
            gsl = slice(g * gw, (g + 1) * gw)
            o_ref[i, :, gsl] = _rms(yz[:, gsl], nw_ref[:, gsl]).astype(o_ref.dtype)


def _conv_shift_matrix(q):
    s = np.zeros(((SSD_CONV - 1) * q, CONV_TAIL + q), np.float32)
    for k in range(SSD_CONV - 1):
        for t in range(q):
            s[k * q + t, CONV_TAIL + t - (SSD_CONV - 1) + k] = 1.0
    return jnp.asarray(s, BF16)


def _lookahead_specs(n_seq, n_chunks, rows, width_in, width_out):
    x_spec = pl.BlockSpec((n_seq, rows, width_in),
                          lambda s: (0, jnp.minimum(s, n_chunks - 1), 0))
    o_spec = pl.BlockSpec((n_seq, rows, width_out), lambda s: (0, jnp.maximum(s - 1, 0), 0))
    return x_spec, o_spec


def _ssd(x, norm_mix_w, w_ssd, conv_w, conv_b, dt_bias, a_log, d_skip_w, norm_w, expand):
    n_seq, seqlen, d = x.shape
    q = SSD_CHUNK
    n_chunks = seqlen // q
    shift = _conv_shift_matrix(q)
    x_spec, o_spec = _lookahead_specs(n_seq, n_chunks, q, d, SSD_DIM)
    consts = (norm_mix_w, w_ssd, conv_w, conv_b, dt_bias, a_log, d_skip_w, norm_w, expand, shift)
    return pl.pallas_call(
        _ssd_kernel,
        grid=(n_chunks + 1,),
        in_specs=[x_spec] + [_const_spec(c.shape) for c in consts],
        out_specs=o_spec,
        out_shape=jax.ShapeDtypeStruct((n_seq, seqlen, SSD_DIM), BF16),
        scratch_shapes=[pltpu.VMEM((n_seq * q, SSD_PROJ), F32),
                        pltpu.VMEM((n_seq, CONV_TAIL + q, SSD_CONV_CH), BF16),
                        pltpu.VMEM((n_seq, SSD_GROUPS, SSD_STATE, SSD_DIM // SSD_GROUPS), F32),
                        pltpu.VMEM((n_seq, q, SSD_DIM), F32),
                        pltpu.VMEM((n_seq, q, SSD_DIM), F32)],
        compiler_params=_cparams(("arbitrary",)),
        name="ssd",
    )(x, *consts)


HG_PROJ_PIECES = 8
HG_VPU_MIN_HALF = 8


def _hgrn2_levels(c):
    levels = []
    s = c // 2
    while s >= 1:
        levels.append(s)
        s //= 2
    return levels


def _hgrn2_tables(c):
    idx = np.arange(c)
    blocks = [np.tril(np.ones((c, c), np.float32))]
    masks = []
    for s in _hgrn2_levels(c):
        mid = (idx // (2 * s)) * (2 * s) + s
        upper = idx >= mid
        if s < HG_VPU_MIN_HALF:
            a = np.zeros((c, c), np.float32)
            for i in range(c):
                if upper[i]:
                    a[i, mid[i]:i + 1] = 1.0
                else:
                    a[i, i + 1:mid[i]] = 1.0
            blocks.append(a)
        same = (idx[:, None] // (2 * s)) == (idx[None, :] // (2 * s))
        masks.append((same & upper[:, None] & (~upper)[None, :]).astype(np.float32))
    masks.append(np.eye(c, dtype=np.float32))
    sums = np.concatenate(blocks, axis=0)
    sums2 = np.concatenate([sums, sums], axis=1)
    masks = np.stack(masks)
    masks2 = np.concatenate([masks, masks], axis=2)
    return jnp.asarray(sums2, BF16), jnp.asarray(masks2, F32)


def _hgrn2_kernel(x_ref, nmw_ref, w_ref, lbp_ref, nw_ref, sums_ref, mask_ref, o_ref,
                  proj_ref, state_ref, g2_ref, q_ref, k_ref, b_ref, ql_ref, klt_ref, qe_ref,
                  ke_ref, v_ref, gate_ref):
    n_seq, c = x_ref.shape[0], x_ref.shape[1]
    levels = _hgrn2_levels(c)
    n_levels = len(levels)
    step = pl.program_id(0)
    seqs = range(n_seq)

    @pl.when(step == 0)
    def _():
        proj_ref[...] = jnp.zeros_like(proj_ref)

    @pl.when(step <= 1)
    def _():
        state_ref[...] = jnp.zeros_like(state_ref)

    lbp = lbp_ref[...]
    e = jnp.exp(lbp - jnp.max(lbp, axis=0, keepdims=True))
    lb = e[0:1, :] / jnp.sum(e, axis=0, keepdims=True)

    for i in seqs:
        rows = slice(i * c, (i + 1) * c)
        q_ref[i] = _silu(proj_ref[rows, 0:HG_KDIM])
        fg = lb + (1.0 - lb) * _sigmoid(proj_ref[rows, HG_KDIM:2 * HG_KDIM])
        v_ref[i] = proj_ref[rows, 2 * HG_KDIM:2 * HG_KDIM + HG_VDIM].astype(BF16)
        gate_ref[i] = _silu(proj_ref[rows, 2 * HG_KDIM + HG_VDIM:])
        g_hi, g_lo = _split_hi_lo(jnp.log2(fg))
        g2_ref[i, 0:c, :] = g_hi
        g2_ref[i, c:2 * c, :] = g_lo
        k_ref[i] = 1.0 - fg

    h_next = _rms(x_ref[...].reshape(n_seq * c, D_MODEL), nmw_ref[...]).astype(BF16)
    piece = HG_PROJ // HG_PROJ_PIECES

    def project(j):
        cols = slice(j * piece, (j + 1) * piece)
        proj_ref[:, cols] = _dot(h_next, w_ref[:, cols])

    project(0)
    project(1)
    fine = []
    for i in seqs:
        fine.append(_dot(sums_ref[...], g2_ref[i]))
        b_ref[i] = fine[i][0:c]
    state_decay = []
    for i in seqs:
        b_last = b_ref[i, c - 1:c, :]
        qe_ref[i] = (q_ref[i] * jnp.exp2(b_ref[i])).astype(BF16)
        ke_ref[i] = (k_ref[i] * jnp.exp2(b_last - b_ref[i])).astype(BF16)
        state_decay.append(jnp.exp2(b_last))

    n_fine = 0
    for lvl, s in enumerate(levels):
        if s < HG_VPU_MIN_HALF:
            n_fine += 1
        for i in seqs:
            if s >= HG_VPU_MIN_HALF:
                q_rows, k_rows = [], []
                for r0 in range(0, c, 2 * s):
                    lo = slice(r0, r0 + s)
                    up = slice(r0 + s, r0 + 2 * s)
                    ref = b_ref[i, r0 + s - 1:r0 + s, :]
                    zeros = jnp.zeros((s, HG_KDIM), F32)
                    q_rows += [zeros, q_ref[i, up, :] * jnp.exp2(b_ref[i, up, :] - ref)]
                    k_rows += [k_ref[i, lo, :] * jnp.exp2(ref - b_ref[i, lo, :]), zeros]
                ql = jnp.concatenate(q_rows, axis=0)
                kl = jnp.concatenate(k_rows, axis=0)
            else:
                ex = jnp.exp2(fine[i][n_fine * c:(n_fine + 1) * c])
                ql = q_ref[i] * ex
                kl = k_ref[i] * ex
            ql_ref[i, lvl] = ql.astype(BF16)
            klt_ref[i, lvl] = kl.T
        if lvl < HG_PROJ_PIECES - 2:
            project(2 + lvl)
    for i in seqs:
        ql_ref[i, n_levels] = q_ref[i].astype(BF16)
        klt_ref[i, n_levels] = k_ref[i].T

    zero = jnp.zeros((HG_K, c), BF16)
    for pair in range(HG_HEADS // 2):
        psl = slice(2 * pair * HG_K, (2 * pair + 2) * HG_K)
        for i in seqs:
            att = None
            for lvl in range(n_levels + 1):
                kt = klt_ref[i, lvl, psl, :].astype(BF16)
                kbd = jnp.concatenate(
                    [jnp.concatenate([kt[:HG_K], zero], axis=1),
                     jnp.concatenate([zero, kt[HG_K:]], axis=1)], axis=0)
                p = _dot(ql_ref[i, lvl, :, psl], kbd)
                att = p if lvl == 0 else att + p * mask_ref[lvl]
            for sub in range(2):
                h = 2 * pair + sub
                sl = slice(h * HG_K, (h + 1) * HG_K)
                vh = v_ref[i, :, sl]
                s_t = state_ref[i, h]
                o = _dot_nt(qe_ref[i, :, sl], s_t.astype(BF16))
                o = o + _dot(att[:, sub * c:(sub + 1) * c].astype(BF16), vh)
                state_ref[i, h] = s_t * state_decay[i][:, sl] + _dot_tn(vh, ke_ref[i, :, sl])
                o_ref[i, :, sl] = (_rms(o, nw_ref[...]) * gate_ref[i, :, sl]).astype(o_ref.dtype)


def _hgrn2(x, norm_mix_w, w_hg, lb_params, norm_w):
    n_seq, seqlen, d = x.shape
    c = HG_CHUNK
    n_chunks = seqlen // c
    n_levels = len(_hgrn2_levels(c))
    sums, masks = _hgrn2_tables(c)
    consts = (norm_mix_w, w_hg, lb_params, norm_w, sums, masks)
    x_spec, o_spec = _lookahead_specs(n_seq, n_chunks, c, d, HG_VDIM)
    wide = lambda dt, *lead: pltpu.VMEM((n_seq, *lead, c, HG_KDIM), dt)
    return pl.pallas_call(
        _hgrn2_kernel,
        grid=(n_chunks + 1,),
        in_specs=[x_spec] + [_const_spec(a.shape) for a in consts],
        out_specs=o_spec,
        out_shape=jax.ShapeDtypeStruct((n_seq, seqlen, HG_VDIM), BF16),
        scratch_shapes=[pltpu.VMEM((n_seq * c, HG_PROJ), F32),
                        pltpu.VMEM((n_seq, HG_HEADS, HG_V, HG_K), F32),
                        pltpu.VMEM((n_seq, 2 * c, HG_KDIM), BF16),
                        wide(F32), wide(F32), wide(F32),
                        wide(BF16, n_levels + 1),
                        pltpu.VMEM((n_seq, n_levels + 1, HG_KDIM, c), F32),
                        wide(BF16), wide(BF16),
                        wide(BF16), wide(F32)],
        compiler_params=_cparams(("arbitrary",)),
        name="hgrn2",
    )(x, *consts)


def _attn_kernel(x_ref, ya_ref, ob_ref, wout_ref, nw_ref, wq_ref, km_ref, vm_ref, wo_ref,
                 o_ref, ox_ref):
    x1 = x_ref[...] + _dot(ya_ref[...], wout_ref[0:SSD_DIM, :]) \
        + _dot(ob_ref[...], wout_ref[SSD_DIM:, :])
    h = _rms(x1, nw_ref[...]).astype(BF16)
    qx = (_dot(h, wq_ref[...]) * (XA_HEAD_DIM ** -0.5)).astype(BF16)
    for hd in range(XA_HEADS):
        sl = slice(hd * XA_HEAD_DIM, (hd + 1) * XA_HEAD_DIM)
        sc = _dot_nt(qx[:, sl], km_ref[0, :, sl])
        p = jnp.exp(sc - jnp.max(sc, axis=-1, keepdims=True))
        denom = jnp.sum(p, axis=-1, keepdims=True)
        ox_ref[:, sl] = (_dot(p.astype(BF16), vm_ref[0, :, sl]) / denom).astype(BF16)
    o_ref[...] = x1 + _dot(ox_ref[...], wo_ref[...])


def _attn(x2d, ya, ob, w_out, norm_w, wq, km, vm, wo, seqlen):
    t, d = x2d.shape
    tm = min(ATTN_ROWS, seqlen)
    tpb = seqlen // tm
    mlen = km.shape[1]
    row = lambda n: pl.BlockSpec((tm, n), lambda i: (i, 0))
    mem = pl.BlockSpec((1, mlen, d), lambda i: (i // tpb, 0, 0))
    return pl.pallas_call(
        _attn_kernel,
        grid=(t // tm,),
        in_specs=[row(d), row(SSD_DIM), row(HG_VDIM), _const_spec(w_out.shape),
                  _const_spec(norm_w.shape), _const_spec(wq.shape), mem, mem,
                  _const_spec(wo.shape)],
        out_specs=row(d),
        out_shape=jax.ShapeDtypeStruct((t, d), F32),
        scratch_shapes=[pltpu.VMEM((tm, d), BF16)],
        compiler_params=_cparams(("arbitrary",)),
        name="attn",
    )(x2d, ya, ob, w_out, norm_w, wq, km, vm, wo)


def _ffn_kernel(x_ref, nw_ref, wg_ref, wu_ref, wd_ref, nf_ref, o_ref, acc_ref):
    x = x_ref[...]
    h = _rms(x, nw_ref[...]).astype(BF16)
    acc_ref[...] = x
    n_cols = wg_ref.shape[1]
    for c0 in range(0, n_cols, FFN_COLS):
        g = _dot(h, wg_ref[:, c0:c0 + FFN_COLS])
        u = _dot(h, wu_ref[:, c0:c0 + FFN_COLS])
        a = (_silu(g) * u).astype(BF16)
        acc_ref[...] += _dot(a, wd_ref[c0:c0 + FFN_COLS, :])
    o_ref[...] = _rms(acc_ref[...], nf_ref[...])


def _ffn(x2d, norm_w, wg, wu, wd, norm_final):
    t, d = x2d.shape
    tm = min(FFN_ROWS, t)
    row = pl.BlockSpec((tm, d), lambda i: (i, 0))
    return pl.pallas_call(
        _ffn_kernel,
        grid=(t // tm,),
        in_specs=[row, _const_spec(norm_w.shape), _const_spec(wg.shape), _const_spec(wu.shape),
                  _const_spec(wd.shape), _const_spec(norm_final.shape)],
        out_specs=row,
        out_shape=jax.ShapeDtypeStruct((t, d), F32),
        scratch_shapes=[pltpu.VMEM((tm, d), F32)],
        compiler_params=_cparams(("arbitrary",)),
        name="ffn",
    )(x2d, norm_w, wg, wu, wd, norm_final)


def _pad_lanes(v):
    return jnp.pad(v.astype(F32), (0, LANES - v.shape[0])).reshape(1, LANES)


def kernel(x, mem, norm_mix_w, w_in, conv_w, conv_b, dt_bias, a_log, d_skip, ssd_norm_w,
           hg_lower_bounds, hg_norm_w, w_out, norm_xa_w, norm_mem_w, xa_wq, xa_wkv, xa_wo,
           norm_ffn_w, ffn_w_gate, ffn_w_up, ffn_w_down, norm_final_w):
    bsz, seqlen, d = x.shape
    assert d == D_MODEL and norm_mix_w.shape[0] == 1, "single-layer block of width 1024"
    assert hg_lower_bounds.shape[0] == 2
    assert seqlen % SSD_CHUNK == 0 and seqlen % HG_CHUNK == 0
    t = bsz * seqlen
    x2d = x.reshape(t, d)
    row = lambda v: v.reshape(1, -1).astype(F32)

    s3 = SSD_DIM + SSD_CONV_CH + SSD_HEADS
    w = w_in[0]
    w_ssd = jnp.pad(w[:, :s3], ((0, 0), (0, LANES - SSD_HEADS))).astype(BF16)
    w_hg = w[:, s3:].astype(BF16)

    expand = (np.arange(LANES)[:, None] == (np.arange(SSD_DIM)[None, :] // SSD_HEAD_DIM))
    expand = jnp.asarray(expand, BF16)
    d_skip_w = jnp.repeat(d_skip[0].astype(F32), SSD_HEAD_DIM).reshape(1, SSD_DIM)
    y_a = _ssd(x, row(norm_mix_w[0]), w_ssd, conv_w[0].astype(F32), row(conv_b[0]),
               _pad_lanes(dt_bias[0]), _pad_lanes(a_log[0]), d_skip_w, row(ssd_norm_w[0]),
               expand).reshape(t, SSD_DIM)
    o_b = _hgrn2(x, row(norm_mix_w[0]), w_hg, hg_lower_bounds.astype(F32),
                 row(hg_norm_w[0])).reshape(t, HG_VDIM)

    km, vm = _mem_kv(mem, row(norm_mem_w[0]), xa_wkv[0].astype(BF16))
    x2 = _attn(x2d, y_a, o_b, w_out[0].astype(BF16), row(norm_xa_w[0]), xa_wq[0].astype(BF16),
               km, vm, xa_wo[0].astype(BF16), seqlen)
    out = _ffn(x2, row(norm_ffn_w[0]), ffn_w_gate[0].astype(BF16), ffn_w_up[0].astype(BF16),
               ffn_w_down[0].astype(BF16), row(norm_final_w))
    return out.reshape(bsz, seqlen, d)
```

```python
import math

import numpy as np
import jax
import jax.numpy as jnp
from jax import lax
from jax.experimental import pallas as pl
from jax.experimental.pallas import tpu as pltpu

F32 = jnp.float32
BF16 = jnp.bfloat16
EPS = 1e-6
LOG2E = math.log2(math.e)

D_MODEL = 1024
SSD_HEADS = 16
SSD_HEAD_DIM = 64
SSD_DIM = SSD_HEADS * SSD_HEAD_DIM
SSD_GROUPS = 2
SSD_STATE = 128
SSD_CONV = 4
SSD_CHUNK = 128
SSD_CONV_CH = SSD_DIM + 2 * SSD_GROUPS * SSD_STATE
HG_HEADS = 8
HG_K = 128
HG_V = 128
HG_KDIM = HG_HEADS * HG_K
HG_VDIM = HG_HEADS * HG_V
XA_HEADS = 4
XA_HEAD_DIM = D_MODEL // XA_HEADS

LANES = 128
BF16_SUBLANES = 16
CONV_TAIL = BF16_SUBLANES
VMEM_LIMIT_BYTES = 56 * 1024 * 1024

HG_CHUNK = 128
ATTN_ROWS = 512
FFN_ROWS = 512
FFN_COLS = 256

SSD_PROJ = SSD_DIM + SSD_CONV_CH + LANES
SSD_PROJ_PIECE = 512
HG_PROJ = 2 * HG_KDIM + 2 * HG_VDIM


def _cparams(sem):
    return pltpu.CompilerParams(dimension_semantics=sem, vmem_limit_bytes=VMEM_LIMIT_BYTES)


def _const_spec(shape):
    zeros = (0,) * len(shape)
    return pl.BlockSpec(shape, lambda *_: zeros)


def _dot(a, b):
    return jnp.dot(a, b, preferred_element_type=F32)


def _dot_nt(a, b):
    return lax.dot_general(a, b, (((1,), (1,)), ((), ())), preferred_element_type=F32)


def _dot_tn(a, b):
    return lax.dot_general(a, b, (((0,), (0,)), ((), ())), preferred_element_type=F32)


def _sigmoid(x):
    return 1.0 / (1.0 + jnp.exp(-x))


def _silu(x):
    return x * _sigmoid(x)


def _rms(x, w):
    ms = jnp.mean(x * x, axis=-1, keepdims=True)
    return x * lax.rsqrt(ms + EPS) * w


def _split_hi_lo(x):
    hi = x.astype(BF16)
    lo = (x - hi.astype(F32)).astype(BF16)
    return hi, lo


def _mem_kv_kernel(mem_ref, nw_ref, wkv_ref, k_ref, v_ref):
    m = _rms(mem_ref[0], nw_ref[...]).astype(BF16)
    kv = _dot(m, wkv_ref[...])
    k_ref[0] = kv[:, :D_MODEL].astype(BF16)
    v_ref[0] = kv[:, D_MODEL:].astype(BF16)


def _mem_kv(mem, norm_w, wkv):
    bsz, mlen, d = mem.shape
    out = jax.ShapeDtypeStruct((bsz, mlen, d), BF16)
    return pl.pallas_call(
        _mem_kv_kernel,
        grid=(bsz,),
        in_specs=[pl.BlockSpec((1, mlen, d), lambda b: (b, 0, 0)),
                  _const_spec((1, d)), _const_spec((d, 2 * d))],
        out_specs=[pl.BlockSpec((1, mlen, d), lambda b: (b, 0, 0))] * 2,
        out_shape=[out, out],
        compiler_params=_cparams(("arbitrary",)),
        name="mem_kv",
    )(mem, norm_w, wkv)


def _ssd_kernel(x_ref, nmw_ref, w_ref, convw_ref, convb_ref, dtb_ref, alog_ref,
                dskip_ref, nw_ref, expand_ref, shift_ref, o_ref,
                proj_ref, ext_ref, state_ref, y_ref, zs_ref):
    n_seq, q = x_ref.shape[0], x_ref.shape[1]
    gw = SSD_DIM // SSD_GROUPS
    heads_per_group = SSD_HEADS // SSD_GROUPS
    step = pl.program_id(0)
    seqs = range(n_seq)

    @pl.when(step == 0)
    def _():
        proj_ref[...] = jnp.zeros_like(proj_ref)

    @pl.when(step <= 1)
    def _():
        ext_ref[:, 0:CONV_TAIL, :] = jnp.zeros((n_seq, CONV_TAIL, SSD_CONV_CH), BF16)
        state_ref[...] = jnp.zeros_like(state_ref)

    @pl.when(step > 1)
    def _():
        ext_ref[:, 0:CONV_TAIL, :] = ext_ref[:, q:q + CONV_TAIL, :]

    row = lax.broadcasted_iota(jnp.int32, (q, q), 0)
    colm = lax.broadcasted_iota(jnp.int32, (q, q), 1)
    causal = row >= colm
    tri = causal.astype(BF16)
    lane = lax.broadcasted_iota(jnp.int32, (q, LANES), 1)
    left = lane < SSD_HEAD_DIM

    h_next = _rms(x_ref[...].reshape(n_seq * q, D_MODEL), nmw_ref[...]).astype(BF16)
    n_pieces = -(-SSD_PROJ // SSD_PROJ_PIECE)

    def project(j):
        cols = slice(j * SSD_PROJ_PIECE, min((j + 1) * SSD_PROJ_PIECE, SSD_PROJ))
        proj_ref[:, cols] = _dot(h_next, w_ref[:, cols])

    xs, xdt, xw, b_all, c_all, acum, acum_t, dfs_w, cd_w = ([None] * n_seq for _ in range(9))
    u_cur, dtr = [None] * n_seq, [None] * n_seq
    for i in seqs:
        rows = slice(i * q, (i + 1) * q)
        zs_ref[i] = _silu(proj_ref[rows, 0:SSD_DIM])
        u_cur[i] = proj_ref[rows, SSD_DIM:SSD_DIM + SSD_CONV_CH].astype(BF16)
        ext_ref[i, CONV_TAIL:CONV_TAIL + q, :] = u_cur[i]
        dtr[i] = proj_ref[rows, SSD_DIM + SSD_CONV_CH:] + dtb_ref[...]

    for i in seqs:
        shifted = _dot(shift_ref[...], ext_ref[i])
        project(2 * i)
        acc = convb_ref[...] + convw_ref[SSD_CONV - 1:SSD_CONV, :] * u_cur[i].astype(F32)
        for k in range(SSD_CONV - 1):
            acc = acc + convw_ref[k:k + 1, :] * shifted[k * q:(k + 1) * q]
        xbc = _silu(acc)
        xs[i] = xbc[:, :SSD_DIM]
        b_all[i] = xbc[:, SSD_DIM:SSD_DIM + SSD_GROUPS * SSD_STATE]
        c_all[i] = xbc[:, SSD_DIM + SSD_GROUPS * SSD_STATE:]

        dt = jnp.maximum(dtr[i], 0.0) + jnp.log(1.0 + jnp.exp(-jnp.abs(dtr[i])))
        da = dt * (-jnp.exp(alog_ref[...]) * LOG2E)
        da_hi, da_lo = _split_hi_lo(da)
        da_lo2 = (da - da_hi.astype(F32) - da_lo.astype(F32)).astype(BF16)
        acum[i] = _dot(tri, da_hi) + _dot(tri, da_lo) + _dot(tri, da_lo2)
        project(2 * i + 1)
        acum_t[i] = acum[i].T
        a_last = acum[i][q - 1:q, :]
        dec_from_start = jnp.exp2(acum[i])
        dec_to_end = jnp.exp2(a_last - acum[i])
        chunk_decay = jnp.broadcast_to(jnp.exp2(a_last), (BF16_SUBLANES, LANES))

        stack = jnp.concatenate([dt, dec_from_start, dec_to_end], axis=0).astype(BF16)
        wide = _dot(stack, expand_ref[...])
        cd_hi, cd_lo = _split_hi_lo(chunk_decay)
        cd_w[i] = (_dot(cd_hi, expand_ref[...]) + _dot(cd_lo, expand_ref[...]))[0:1]
        dfs_w[i] = wide[q:2 * q]
        xdt[i] = xs[i] * wide[0:q]
        xw[i] = (xdt[i] * wide[2 * q:3 * q]).astype(BF16)

    for g in range(SSD_GROUPS):
        gsl = slice(g * gw, (g + 1) * gw)
        for i in seqs:
            b_g = b_all[i][:, g * SSD_STATE:(g + 1) * SSD_STATE].astype(BF16)
            c_g = c_all[i][:, g * SSD_STATE:(g + 1) * SSD_STATE].astype(BF16)
            cb = _dot_nt(c_g, b_g)
            s_in = state_ref[i, g]
            y_ref[i, :, gsl] = _dot(c_g, s_in.astype(BF16)) * dfs_w[i][:, gsl]
            state_ref[i, g] = s_in * cd_w[i][:, gsl] + _dot_tn(b_g, xw[i][:, gsl])
            for pair in range(heads_per_group // 2):
                h0 = g * heads_per_group + 2 * pair
                ms = []
                for h in (h0, h0 + 1):
                    seg = acum[i][:, h:h + 1] - acum_t[i][h:h + 1, :]
                    l_dec = jnp.exp2(jnp.where(causal, seg, -jnp.inf))
                    ms.append((cb * l_dec).astype(BF16))
                lhs = jnp.concatenate(ms, axis=1)
                psl = slice(h0 * SSD_HEAD_DIM, (h0 + 2) * SSD_HEAD_DIM)
                xp = xdt[i][:, psl]
                rhs = jnp.concatenate([jnp.where(left, xp, 0.0), jnp.where(left, 0.0, xp)],
                                      axis=0).astype(BF16)
                y_ref[i, :, psl] += _dot(lhs, rhs)

    for j in range(2 * n_seq, n_pieces):
        project(j)
    for i in seqs:
        yz = (y_ref[i] + dskip_ref[...] * xs[i]) * zs_ref[i]
        for g in range(SSD_GROUPS):
            gsl = slice(g * gw, (g + 1) * gw)
            o_ref[i, :, gsl] = _rms(yz[:, gsl], nw_ref[:, gsl]).astype(o_ref.dtype)


def _conv_shift_matrix(q):
    s = np.zeros(((SSD_CONV - 1) * q, CONV_TAIL + q), np.float32)
    for k in range(SSD_CONV - 1):
        for t in range(q):
            s[k * q + t, CONV_TAIL + t - (SSD_CONV - 1) + k] = 1.0
    return jnp.asarray(s, BF16)


def _lookahead_specs(n_seq, n_chunks, rows, width_in, width_out):
    x_spec = pl.BlockSpec((n_seq, rows, width_in),
                          lambda s: (0, jnp.minimum(s, n_chunks - 1), 0))
    o_spec = pl.BlockSpec((n_seq, rows, width_out), lambda s: (0, jnp.maximum(s - 1, 0), 0))
    return x_spec, o_spec


def _ssd(x, norm_mix_w, w_ssd, conv_w, conv_b, dt_bias, a_log, d_skip_w, norm_w, expand):
    n_seq, seqlen, d = x.shape
    q = SSD_CHUNK
    n_chunks = seqlen // q
    shift = _conv_shift_matrix(q)
    x_spec, o_spec = _lookahead_specs(n_seq, n_chunks, q, d, SSD_DIM)
    consts = (norm_mix_w, w_ssd, conv_w, conv_b, dt_bias, a_log, d_skip_w, norm_w, expand, shift)
    return pl.pallas_call(
        _ssd_kernel,
        grid=(n_chunks + 1,),
        in_specs=[x_spec] + [_const_spec(c.shape) for c in consts],
        out_specs=o_spec,
        out_shape=jax.ShapeDtypeStruct((n_seq, seqlen, SSD_DIM), BF16),
        scratch_shapes=[pltpu.VMEM((n_seq * q, SSD_PROJ), F32),
                        pltpu.VMEM((n_seq, CONV_TAIL + q, SSD_CONV_CH), BF16),
                        pltpu.VMEM((n_seq, SSD_GROUPS, SSD_STATE, SSD_DIM // SSD_GROUPS), F32),
                        pltpu.VMEM((n_seq, q, SSD_DIM), F32),
                        pltpu.VMEM((n_seq, q, SSD_DIM), F32)],
        compiler_params=_cparams(("arbitrary",)),
        name="ssd",
    )(x, *consts)


HG_PROJ_PIECES = 8
HG_VPU_MIN_HALF = 8


def _hgrn2_levels(c):
    levels = []
    s = c // 2
    while s >= 1:
        levels.append(s)
        s //= 2
    return levels


def _hgrn2_tables(c):
    idx = np.arange(c)
    blocks = [np.tril(np.ones((c, c), np.float32))]
    masks = []
    for s in _hgrn2_levels(c):
        mid = (idx // (2 * s)) * (2 * s) + s
        upper = idx >= mid
        if 1 < s < HG_VPU_MIN_HALF:
            a = np.zeros((c, c), np.float32)
            for i in range(c):
                if upper[i]:
                    a[i, mid[i]:i + 1] = 1.0
                else:
                    a[i, i + 1:mid[i]] = 1.0
            blocks.append(a)
        same = (idx[:, None] // (2 * s)) == (idx[None, :] // (2 * s))
        masks.append((same & upper[:, None] & (~upper)[None, :]).astype(np.float32))
    masks.append(np.eye(c, dtype=np.float32))
    sums = np.concatenate(blocks, axis=0)
    sums2 = np.concatenate([sums, sums], axis=1)
    masks = np.stack(masks)
    masks2 = np.concatenate([masks, masks], axis=2)
    return jnp.asarray(sums2, BF16), jnp.asarray(masks2, F32)


def _hgrn2_kernel(x_ref, nmw_ref, w_ref, lbp_ref, nw_ref, sums_ref, mask_ref, o_ref,
                  proj_ref, state_ref, g2_ref, q_ref, k_ref, b_ref, ql_ref, klt_ref, qe_ref,
                  ke_ref, v_ref, gate_ref):
    n_seq, c = x_ref.shape[0], x_ref.shape[1]
    levels = _hgrn2_levels(c)
    n_levels = len(levels)
    step = pl.program_id(0)
    seqs = range(n_seq)

    @pl.when(step == 0)
    def _():
        proj_ref[...] = jnp.zeros_like(proj_ref)

    @pl.when(step <= 1)
    def _():
        state_ref[...] = jnp.zeros_like(state_ref)

    lbp = lbp_ref[...]
    e = jnp.exp(lbp - jnp.max(lbp, axis=0, keepdims=True))
    lb = e[0:1, :] / jnp.sum(e, axis=0, keepdims=True)

    for i in seqs:
        rows = slice(i * c, (i + 1) * c)
        q_ref[i] = _silu(proj_ref[rows, 0:HG_KDIM])
        fg = lb + (1.0 - lb) * _sigmoid(proj_ref[rows, HG_KDIM:2 * HG_KDIM])
        v_ref[i] = proj_ref[rows, 2 * HG_KDIM:2 * HG_KDIM + HG_VDIM].astype(BF16)
        gate_ref[i] = _silu(proj_ref[rows, 2 * HG_KDIM + HG_VDIM:])
        g_hi, g_lo = _split_hi_lo(jnp.log2(fg))
        g2_ref[i, 0:c, :] = g_hi
        g2_ref[i, c:2 * c, :] = g_lo
        k_ref[i] = 1.0 - fg
        ql_ref[i, n_levels - 1] = (q_ref[i] * fg).astype(BF16)

    h_next = _rms(x_ref[...].reshape(n_seq * c, D_MODEL), nmw_ref[...]).astype(BF16)
    piece = HG_PROJ // HG_PROJ_PIECES

    def project(j):
        cols = slice(j * piece, (j + 1) * piece)
        proj_ref[:, cols] = _dot(h_next, w_ref[:, cols])

    fine, state_decay = [None] * n_seq, [None] * n_seq

    def cumsums(i):
        fine[i] = _dot(sums_ref[...], g2_ref[i])
        b_ref[i] = fine[i][0:c]
        b_last = b_ref[i, c - 1:c, :]
        qe_ref[i] = (q_ref[i] * jnp.exp2(b_ref[i])).astype(BF16)
        ke_ref[i] = (k_ref[i] * jnp.exp2(b_last - b_ref[i])).astype(BF16)
        state_decay[i] = jnp.exp2(b_last)

    def level(i, lvl):
        if lvl == n_levels - 1:
            return
        if lvl == n_levels:
            ql, kl = q_ref[i], k_ref[i]
        elif levels[lvl] >= HG_VPU_MIN_HALF:
            s = levels[lvl]
            q_rows, k_rows = [], []
            for r0 in range(0, c, 2 * s):
                lo = slice(r0, r0 + s)
                up = slice(r0 + s, r0 + 2 * s)
                ref = b_ref[i, r0 + s - 1:r0 + s, :]
                zeros = jnp.zeros((s, HG_KDIM), F32)
                q_rows += [zeros, q_ref[i, up, :] * jnp.exp2(b_ref[i, up, :] - ref)]
                k_rows += [k_ref[i, lo, :] * jnp.exp2(ref - b_ref[i, lo, :]), zeros]
            ql = jnp.concatenate(q_rows, axis=0)
            kl = jnp.concatenate(k_rows, axis=0)
        else:
            n_fine = 1 + sum(1 for s in levels[:lvl] if 1 < s < HG_VPU_MIN_HALF)
            ex = jnp.exp2(fine[i][n_fine * c:(n_fine + 1) * c])
            ql = q_ref[i] * ex
            kl = k_ref[i] * ex
        ql_ref[i, lvl] = ql.astype(BF16)
        klt_ref[i, lvl] = kl.T

    zero = jnp.zeros((HG_K, c), BF16)

    def head_pair(i, pair):
        psl = slice(2 * pair * HG_K, (2 * pair + 2) * HG_K)
        def block_diag(lvl):
            kt = klt_ref[i, lvl, psl, :].astype(BF16)
            return jnp.concatenate(
                [jnp.concatenate([kt[:HG_K], zero], axis=1),
                 jnp.concatenate([zero, kt[HG_K:]], axis=1)], axis=0)

        att = _dot(ql_ref[i, 0, :, psl], block_diag(0))
        for lvl in range(1, n_levels - 1):
            att = att + _dot(ql_ref[i, lvl, :, psl], block_diag(lvl)) * mask_ref[lvl]
        kbd = block_diag(n_levels)
        for lvl in (n_levels - 1, n_levels):
            att = att + _dot(ql_ref[i, lvl, :, psl], kbd) * mask_ref[lvl]
        for sub in range(2):
            h = 2 * pair + sub
            sl = slice(h * HG_K, (h + 1) * HG_K)
            vh = v_ref[i, :, sl]
            s_t = state_ref[i, h]
            o = _dot_nt(qe_ref[i, :, sl], s_t.astype(BF16))
            o = o + _dot(att[:, sub * c:(sub + 1) * c].astype(BF16), vh)
            state_ref[i, h] = s_t * state_decay[i][:, sl] + _dot_tn(vh, ke_ref[i, :, sl])
            o_ref[i, :, sl] = (_rms(o, nw_ref[...]) * gate_ref[i, :, sl]).astype(o_ref.dtype)

    project(0)
    project(1)
    for i in seqs:
        cumsums(i)
    n_pairs = HG_HEADS // 2
    per_pair = -(-(n_levels + 1) // n_pairs)
    for lvl in range(n_levels + 1):
        level(0, lvl)
        if 2 + lvl < HG_PROJ_PIECES:
            project(2 + lvl)
    for i in seqs:
        for pair in range(n_pairs):
            head_pair(i, pair)
            if i + 1 < n_seq:
                for lvl in range(pair * per_pair, min((pair + 1) * per_pair, n_levels + 1)):
                    level(i + 1, lvl)


def _hgrn2(x, norm_mix_w, w_hg, lb_params, norm_w):
    n_seq, seqlen, d = x.shape
    c = HG_CHUNK
    n_chunks = seqlen // c
    n_levels = len(_hgrn2_levels(c))
    sums, masks = _hgrn2_tables(c)
    consts = (norm_mix_w, w_hg, lb_params, norm_w, sums, masks)
    x_spec, o_spec = _lookahead_specs(n_seq, n_chunks, c, d, HG_VDIM)
    wide = lambda dt, *lead: pltpu.VMEM((n_seq, *lead, c, HG_KDIM), dt)
    return pl.pallas_call(
        _hgrn2_kernel,
        grid=(n_chunks + 1,),
        in_specs=[x_spec] + [_const_spec(a.shape) for a in consts],
        out_specs=o_spec,
        out_shape=jax.ShapeDtypeStruct((n_seq, seqlen, HG_VDIM), BF16),
        scratch_shapes=[pltpu.VMEM((n_seq * c, HG_PROJ), F32),
                        pltpu.VMEM((n_seq, HG_HEADS, HG_V, HG_K), F32),
                        pltpu.VMEM((n_seq, 2 * c, HG_KDIM), BF16),
                        wide(F32), wide(F32), wide(F32),
                        wide(BF16, n_levels + 1),
                        pltpu.VMEM((n_seq, n_levels + 1, HG_KDIM, c), F32),
                        wide(BF16), wide(BF16),
                        wide(BF16), wide(F32)],
        compiler_params=_cparams(("arbitrary",)),
        name="hgrn2",
    )(x, *consts)


def _attn_kernel(x_ref, ya_ref, ob_ref, wout_ref, nw_ref, wq_ref, km_ref, vm_ref, wo_ref,
                 o_ref, ox_ref):
    x1 = x_ref[...] + _dot(ya_ref[...], wout_ref[0:SSD_DIM, :]) \
        + _dot(ob_ref[...], wout_ref[SSD_DIM:, :])
    h = _rms(x1, nw_ref[...]).astype(BF16)
    qx = (_dot(h, wq_ref[...]) * (XA_HEAD_DIM ** -0.5)).astype(BF16)
    for hd in range(XA_HEADS):
        sl = slice(hd * XA_HEAD_DIM, (hd + 1) * XA_HEAD_DIM)
        sc = _dot_nt(qx[:, sl], km_ref[0, :, sl])
        p = jnp.exp(sc - jnp.max(sc, axis=-1, keepdims=True))
        denom = jnp.sum(p, axis=-1, keepdims=True)
        ox_ref[:, sl] = (_dot(p.astype(BF16), vm_ref[0, :, sl]) / denom).astype(BF16)
    o_ref[...] = x1 + _dot(ox_ref[...], wo_ref[...])


def _attn(x2d, ya, ob, w_out, norm_w, wq, km, vm, wo, seqlen):
    t, d = x2d.shape
    tm = min(ATTN_ROWS, seqlen)
    tpb = seqlen // tm
    mlen = km.shape[1]
    row = lambda n: pl.BlockSpec((tm, n), lambda i: (i, 0))
    mem = pl.BlockSpec((1, mlen, d), lambda i: (i // tpb, 0, 0))
    return pl.pallas_call(
        _attn_kernel,
        grid=(t // tm,),
        in_specs=[row(d), row(SSD_DIM), row(HG_VDIM), _const_spec(w_out.shape),
                  _const_spec(norm_w.shape), _const_spec(wq.shape), mem, mem,
                  _const_spec(wo.shape)],
        out_specs=row(d),
        out_shape=jax.ShapeDtypeStruct((t, d), F32),
        scratch_shapes=[pltpu.VMEM((tm, d), BF16)],
        compiler_params=_cparams(("arbitrary",)),
        name="attn",
    )(x2d, ya, ob, w_out, norm_w, wq, km, vm, wo)


def _ffn_kernel(x_ref, nw_ref, wg_ref, wu_ref, wd_ref, nf_ref, o_ref, acc_ref):
    x = x_ref[...]
    h = _rms(x, nw_ref[...]).astype(BF16)
    acc_ref[...] = x
    n_cols = wg_ref.shape[1]
    for c0 in range(0, n_cols, FFN_COLS):
        g = _dot(h, wg_ref[:, c0:c0 + FFN_COLS])
        u = _dot(h, wu_ref[:, c0:c0 + FFN_COLS])
        a = (_silu(g) * u).astype(BF16)
        acc_ref[...] += _dot(a, wd_ref[c0:c0 + FFN_COLS, :])
    o_ref[...] = _rms(acc_ref[...], nf_ref[...])


def _ffn(x2d, norm_w, wg, wu, wd, norm_final):
    t, d = x2d.shape
    tm = min(FFN_ROWS, t)
    row = pl.BlockSpec((tm, d), lambda i: (i, 0))
    return pl.pallas_call(
        _ffn_kernel,
        grid=(t // tm,),
        in_specs=[row, _const_spec(norm_w.shape), _const_spec(wg.shape), _const_spec(wu.shape),
                  _const_spec(wd.shape), _const_spec(norm_final.shape)],
        out_specs=row,
        out_shape=jax.ShapeDtypeStruct((t, d), F32),
        scratch_shapes=[pltpu.VMEM((tm, d), F32)],
        compiler_params=_cparams(("arbitrary",)),
        name="ffn",
    )(x2d, norm_w, wg, wu, wd, norm_final)


def _pad_lanes(v):
    return jnp.pad(v.astype(F32), (0, LANES - v.shape[0])).reshape(1, LANES)


def kernel(x, mem, norm_mix_w, w_in, conv_w, conv_b, dt_bias, a_log, d_skip, ssd_norm_w,
           hg_lower_bounds, hg_norm_w, w_out, norm_xa_w, norm_mem_w, xa_wq, xa_wkv, xa_wo,
           norm_ffn_w, ffn_w_gate, ffn_w_up, ffn_w_down, norm_final_w):
    bsz, seqlen, d = x.shape
    assert d == D_MODEL and norm_mix_w.shape[0] == 1, "single-layer block of width 1024"
    assert hg_lower_bounds.shape[0] == 2
    assert seqlen % SSD_CHUNK == 0 and seqlen % HG_CHUNK == 0
    t = bsz * seqlen
    x2d = x.reshape(t, d)
    row = lambda v: v.reshape(1, -1).astype(F32)

    s3 = SSD_DIM + SSD_CONV_CH + SSD_HEADS
    w = w_in[0]
    w_ssd = jnp.pad(w[:, :s3], ((0, 0), (0, LANES - SSD_HEADS))).astype(BF16)
    w_hg = w[:, s3:].astype(BF16)

    expand = (np.arange(LANES)[:, None] == (np.arange(SSD_DIM)[None, :] // SSD_HEAD_DIM))
    expand = jnp.asarray(expand, BF16)
    d_skip_w = jnp.repeat(d_skip[0].astype(F32), SSD_HEAD_DIM).reshape(1, SSD_DIM)
    y_a = _ssd(x, row(norm_mix_w[0]), w_ssd, conv_w[0].astype(F32), row(conv_b[0]),
               _pad_lanes(dt_bias[0]), _pad_lanes(a_log[0]), d_skip_w, row(ssd_norm_w[0]),
               expand).reshape(t, SSD_DIM)
    o_b = _hgrn2(x, row(norm_mix_w[0]), w_hg, hg_lower_bounds.astype(F32),
                 row(hg_norm_w[0])).reshape(t, HG_VDIM)

    km, vm = _mem_kv(mem, row(norm_mem_w[0]), xa_wkv[0].astype(BF16))
    x2 = _attn(x2d, y_a, o_b, w_out[0].astype(BF16), row(norm_xa_w[0]), xa_wq[0].astype(BF16),
               km, vm, xa_wo[0].astype(BF16), seqlen)
    out = _ffn(x2, row(norm_ffn_w[0]), ffn_w_gate[0].astype(BF16), ffn_w_up[0].astype(BF16),
               ffn_w_down[0].astype(BF16), row(norm_final_w))
    return out.reshape(bsz, seqlen, d)
```

```python
import math

import numpy as np
import jax
import jax.numpy as jnp
from jax import lax
from jax.experimental import pallas as pl
from jax.experimental.pallas import tpu as pltpu

F32 = jnp.float32
BF16 = jnp.bfloat16
EPS = 1e-6
LOG2E = math.log2(math.e)

D_MODEL = 1024
SSD_HEADS = 16
SSD_HEAD_DIM = 64
SSD_DIM = SSD_HEADS * SSD_HEAD_DIM
SSD_GROUPS = 2
SSD_STATE = 128
SSD_CONV = 4
SSD_CHUNK = 128
SSD_CONV_CH = SSD_DIM + 2 * SSD_GROUPS * SSD_STATE
HG_HEADS = 8
HG_K = 128
HG_V = 128
HG_KDIM = HG_HEADS * HG_K
HG_VDIM = HG_HEADS * HG_V
XA_HEADS = 4
XA_HEAD_DIM = D_MODEL // XA_HEADS

LANES = 128
BF16_SUBLANES = 16
CONV_TAIL = BF16_SUBLANES
VMEM_LIMIT_BYTES = 56 * 1024 * 1024

HG_CHUNK = 128
ATTN_ROWS = 512
FFN_ROWS = 512
FFN_COLS = 256

SSD_PROJ = SSD_DIM + SSD_CONV_CH + LANES
SSD_PROJ_PIECE = 512
HG_PROJ = 2 * HG_KDIM + 2 * HG_VDIM


def _cparams(sem):
    return pltpu.CompilerParams(dimension_semantics=sem, vmem_limit_bytes=VMEM_LIMIT_BYTES)


def _const_spec(shape):
    zeros = (0,) * len(shape)
    return pl.BlockSpec(shape, lambda *_: zeros)


def _dot(a, b):
    return jnp.dot(a, b, preferred_element_type=F32)


def _dot_nt(a, b):
    return lax.dot_general(a, b, (((1,), (1,)), ((), ())), preferred_element_type=F32)


def _dot_tn(a, b):
    return lax.dot_general(a, b, (((0,), (0,)), ((), ())), preferred_element_type=F32)


def _sigmoid(x):
    return 1.0 / (1.0 + jnp.exp(-x))


def _silu(x):
    return x * _sigmoid(x)


def _rms(x, w):
    ms = jnp.mean(x * x, axis=-1, keepdims=True)
    return x * lax.rsqrt(ms + EPS) * w


def _split_hi_lo(x):
    hi = x.astype(BF16)
    lo = (x - hi.astype(F32)).astype(BF16)
    return hi, lo


def _mem_kv_kernel(mem_ref, nw_ref, wkv_ref, k_ref, v_ref):
    m = _rms(mem_ref[0], nw_ref[...]).astype(BF16)
    kv = _dot(m, wkv_ref[...])
    k_ref[0] = kv[:, :D_MODEL].astype(BF16)
    v_ref[0] = kv[:, D_MODEL:].astype(BF16)


def _mem_kv(mem, norm_w, wkv):
    bsz, mlen, d = mem.shape
    out = jax.ShapeDtypeStruct((bsz, mlen, d), BF16)
    return pl.pallas_call(
        _mem_kv_kernel,
        grid=(bsz,),
        in_specs=[pl.BlockSpec((1, mlen, d), lambda b: (b, 0, 0)),
                  _const_spec((1, d)), _const_spec((d, 2 * d))],
        out_specs=[pl.BlockSpec((1, mlen, d), lambda b: (b, 0, 0))] * 2,
        out_shape=[out, out],
        compiler_params=_cparams(("arbitrary",)),
        name="mem_kv",
    )(mem, norm_w, wkv)


def _ssd_kernel(x_ref, nmw_ref, w_ref, convw_ref, convb_ref, dtb_ref, alog_ref,
                dskip_ref, nw_ref, expand_ref, shift_ref, o_ref,
                proj_ref, ext_ref, state_ref, y_ref, zs_ref):
    n_seq, q = x_ref.shape[0], x_ref.shape[1]
    gw = SSD_DIM // SSD_GROUPS
    heads_per_group = SSD_HEADS // SSD_GROUPS
    step = pl.program_id(0)
    seqs = range(n_seq)

    @pl.when(step == 0)
    def _():
        proj_ref[...] = jnp.zeros_like(proj_ref)

    @pl.when(step <= 1)
    def _():
        ext_ref[:, 0:CONV_TAIL, :] = jnp.zeros((n_seq, CONV_TAIL, SSD_CONV_CH), BF16)
        state_ref[...] = jnp.zeros_like(state_ref)

    @pl.when(step > 1)
    def _():
        ext_ref[:, 0:CONV_TAIL, :] = ext_ref[:, q:q + CONV_TAIL, :]

    row = lax.broadcasted_iota(jnp.int32, (q, q), 0)
    colm = lax.broadcasted_iota(jnp.int32, (q, q), 1)
    causal = row >= colm
    tri = causal.astype(BF16)
    lane = lax.broadcasted_iota(jnp.int32, (q, LANES), 1)
    left = lane < SSD_HEAD_DIM

    h_next = _rms(x_ref[...].reshape(n_seq * q, D_MODEL), nmw_ref[...]).astype(BF16)
    n_pieces = -(-SSD_PROJ // SSD_PROJ_PIECE)

    def project(j):
        cols = slice(j * SSD_PROJ_PIECE, min((j + 1) * SSD_PROJ_PIECE, SSD_PROJ))
        proj_ref[:, cols] = _dot(h_next, w_ref[:, cols])

    xs, xdt, xw, b_all, c_all, acum, acum_t, dfs_w, cd_w = ([None] * n_seq for _ in range(9))
    u_cur, dtr = [None] * n_seq, [None] * n_seq
    for i in seqs:
        rows = slice(i * q, (i + 1) * q)
        zs_ref[i] = _silu(proj_ref[rows, 0:SSD_DIM])
        u_cur[i] = proj_ref[rows, SSD_DIM:SSD_DIM + SSD_CONV_CH].astype(BF16)
        ext_ref[i, CONV_TAIL:CONV_TAIL + q, :] = u_cur[i]
        dtr[i] = proj_ref[rows, SSD_DIM + SSD_CONV_CH:] + dtb_ref[...]

    for i in seqs:
        shifted = _dot(shift_ref[...], ext_ref[i])
        project(2 * i)
        acc = convb_ref[...] + convw_ref[SSD_CONV - 1:SSD_CONV, :] * u_cur[i].astype(F32)
        for k in range(SSD_CONV - 1):
            acc = acc + convw_ref[k:k + 1, :] * shifted[k * q:(k + 1) * q]
        xbc = _silu(acc)
        xs[i] = xbc[:, :SSD_DIM]
        b_all[i] = xbc[:, SSD_DIM:SSD_DIM + SSD_GROUPS * SSD_STATE]
        c_all[i] = xbc[:, SSD_DIM + SSD_GROUPS * SSD_STATE:]

        dt = jnp.maximum(dtr[i], 0.0) + jnp.log(1.0 + jnp.exp(-jnp.abs(dtr[i])))
        da = dt * (-jnp.exp(alog_ref[...]) * LOG2E)
        da_hi, da_lo = _split_hi_lo(da)
        da_lo2 = (da - da_hi.astype(F32) - da_lo.astype(F32)).astype(BF16)
        acum[i] = _dot(tri, da_hi) + _dot(tri, da_lo) + _dot(tri, da_lo2)
        project(2 * i + 1)
        acum_t[i] = acum[i].T
        a_last = acum[i][q - 1:q, :]
        dec_from_start = jnp.exp2(acum[i])
        dec_to_end = jnp.exp2(a_last - acum[i])
        chunk_decay = jnp.broadcast_to(jnp.exp2(a_last), (BF16_SUBLANES, LANES))

        stack = jnp.concatenate([dt, dec_from_start, dec_to_end], axis=0).astype(BF16)
        wide = _dot(stack, expand_ref[...])
        cd_hi, cd_lo = _split_hi_lo(chunk_decay)
        cd_w[i] = (_dot(cd_hi, expand_ref[...]) + _dot(cd_lo, expand_ref[...]))[0:1]
        dfs_w[i] = wide[q:2 * q]
        xdt[i] = xs[i] * wide[0:q]
        xw[i] = (xdt[i] * wide[2 * q:3 * q]).astype(BF16)

    for g in range(SSD_GROUPS):
        gsl = slice(g * gw, (g + 1) * gw)
        for i in seqs:
            b_g = b_all[i][:, g * SSD_STATE:(g + 1) * SSD_STATE].astype(BF16)
            c_g = c_all[i][:, g * SSD_STATE:(g + 1) * SSD_STATE].astype(BF16)
            cb = _dot_nt(c_g, b_g)
            s_in = state_ref[i, g]
            y_ref[i, :, gsl] = _dot(c_g, s_in.astype(BF16)) * dfs_w[i][:, gsl]
            state_ref[i, g] = s_in * cd_w[i][:, gsl] + _dot_tn(b_g, xw[i][:, gsl])
            for pair in range(heads_per_group // 2):
                h0 = g * heads_per_group + 2 * pair
                ms = []
                for h in (h0, h0 + 1):
                    seg = acum[i][:, h:h + 1] - acum_t[i][h:h + 1, :]
                    l_dec = jnp.exp2(jnp.where(causal, seg, -jnp.inf))
                    ms.append((cb * l_dec).astype(BF16))
                lhs = jnp.concatenate(ms, axis=1)
                psl = slice(h0 * SSD_HEAD_DIM, (h0 + 2) * SSD_HEAD_DIM)
                xp = xdt[i][:, psl]
                rhs = jnp.concatenate([jnp.where(left, xp, 0.0), jnp.where(left, 0.0, xp)],
                                      axis=0).astype(BF16)
                y_ref[i, :, psl] += _dot(lhs, rhs)

    for j in range(2 * n_seq, n_pieces):
        project(j)
    for i in seqs:
        yz = (y_ref[i] + dskip_ref[...] * xs[i]) * zs_ref[i]
        for g in range(SSD_GROUPS):
            gsl = slice(g * gw, (g + 1) * gw)
            o_ref[i, :, gsl] = _rms(yz[:, gsl], nw_ref[:, gsl]).astype(o_ref.dtype)


def _conv_shift_matrix(q):
    s = np.zeros(((SSD_CONV - 1) * q, CONV_TAIL + q), np.float32)
    for k in range(SSD_CONV - 1):
        for t in range(q):
            s[k * q + t, CONV_TAIL + t - (SSD_CONV - 1) + k] = 1.0
    return jnp.asarray(s, BF16)


def _lookahead_specs(n_seq, n_chunks, rows, width_in, width_out):
    x_spec = pl.BlockSpec((n_seq, rows, width_in),
                          lambda s: (0, jnp.minimum(s, n_chunks - 1), 0))
    o_spec = pl.BlockSpec((n_seq, rows, width_out), lambda s: (0, jnp.maximum(s - 1, 0), 0))
    return x_spec, o_spec


def _ssd(x, norm_mix_w, w_ssd, conv_w, conv_b, dt_bias, a_log, d_skip_w, norm_w, expand):
    n_seq, seqlen, d = x.shape
    q = SSD_CHUNK
    n_chunks = seqlen // q
    shift = _conv_shift_matrix(q)
    x_spec, o_spec = _lookahead_specs(n_seq, n_chunks, q, d, SSD_DIM)
    consts = (norm_mix_w, w_ssd, conv_w, conv_b, dt_bias, a_log, d_skip_w, norm_w, expand, shift)
    return pl.pallas_call(
        _ssd_kernel,
        grid=(n_chunks + 1,),
        in_specs=[x_spec] + [_const_spec(c.shape) for c in consts],
        out_specs=o_spec,
        out_shape=jax.ShapeDtypeStruct((n_seq, seqlen, SSD_DIM), BF16),
        scratch_shapes=[pltpu.VMEM((n_seq * q, SSD_PROJ), F32),
                        pltpu.VMEM((n_seq, CONV_TAIL + q, SSD_CONV_CH), BF16),
                        pltpu.VMEM((n_seq, SSD_GROUPS, SSD_STATE, SSD_DIM // SSD_GROUPS), F32),
                        pltpu.VMEM((n_seq, q, SSD_DIM), F32),
                        pltpu.VMEM((n_seq, q, SSD_DIM), F32)],
        compiler_params=_cparams(("arbitrary",)),
        name="ssd",
    )(x, *consts)


HG_PROJ_PIECES = 8
HG_VPU_MIN_HALF = 8


def _hgrn2_levels(c):
    levels = []
    s = c // 2
    while s >= 1:
        levels.append(s)
        s //= 2
    return levels


def _hgrn2_tables(c):
    idx = np.arange(c)
    blocks = [np.tril(np.ones((c, c), np.float32))]
    masks = []
    for s in _hgrn2_levels(c):
        mid = (idx // (2 * s)) * (2 * s) + s
        upper = idx >= mid
        if 1 < s < HG_VPU_MIN_HALF:
            a = np.zeros((c, c), np.float32)
            for i in range(c):
                if upper[i]:
                    a[i, mid[i]:i + 1] = 1.0
                else:
                    a[i, i + 1:mid[i]] = 1.0
            blocks.append(a)
        same = (idx[:, None] // (2 * s)) == (idx[None, :] // (2 * s))
        masks.append((same & upper[:, None] & (~upper)[None, :]).astype(np.float32))
    masks.append(np.eye(c, dtype=np.float32))
    sums = np.concatenate(blocks, axis=0)
    sums2 = np.concatenate([sums, sums], axis=1)
    masks = np.stack(masks)
    masks2 = np.concatenate([masks, masks], axis=2)
    return jnp.asarray(sums2, BF16), jnp.asarray(masks2, F32)


def _hgrn2_kernel(x_ref, nmw_ref, w_ref, lbp_ref, nw_ref, sums_ref, mask_ref, o_ref,
                  proj_ref, state_ref, g2_ref, q_ref, k_ref, b_ref, ql_ref, klt_ref, qe_ref,
                  ke_ref, v_ref, gate_ref):
    n_seq, c = x_ref.shape[0], x_ref.shape[1]
    levels = _hgrn2_levels(c)
    n_levels = len(levels)
    step = pl.program_id(0)
    seqs = range(n_seq)

    @pl.when(step == 0)
    def _():
        proj_ref[...] = jnp.zeros_like(proj_ref)

    @pl.when(step <= 1)
    def _():
        state_ref[...] = jnp.zeros_like(state_ref)

    lbp = lbp_ref[...]
    e = jnp.exp(lbp - jnp.max(lbp, axis=0, keepdims=True))
    lb = e[0:1, :] / jnp.sum(e, axis=0, keepdims=True)

    for i in seqs:
        rows = slice(i * c, (i + 1) * c)
        q = _silu(proj_ref[rows, 0:HG_KDIM])
        q_ref[i] = q.astype(BF16)
        fg = lb + (1.0 - lb) * _sigmoid(proj_ref[rows, HG_KDIM:2 * HG_KDIM])
        v_ref[i] = proj_ref[rows, 2 * HG_KDIM:2 * HG_KDIM + HG_VDIM].astype(BF16)
        gate_ref[i] = _silu(proj_ref[rows, 2 * HG_KDIM + HG_VDIM:])
        g_hi, g_lo = _split_hi_lo(jnp.log2(fg))
        g2_ref[i, 0:c, :] = g_hi
        g2_ref[i, c:2 * c, :] = g_lo
        k_ref[i] = (1.0 - fg).astype(BF16)
        ql_ref[i, n_levels - 1] = (q * fg).astype(BF16)

    h_next = _rms(x_ref[...].reshape(n_seq * c, D_MODEL), nmw_ref[...]).astype(BF16)
    piece = HG_PROJ // HG_PROJ_PIECES

    def project(j):
        cols = slice(j * piece, (j + 1) * piece)
        proj_ref[:, cols] = _dot(h_next, w_ref[:, cols])

    fine, state_decay = [None] * n_seq, [None] * n_seq

    def cumsums(i):
        fine[i] = _dot(sums_ref[...], g2_ref[i])
        b_ref[i] = fine[i][0:c]
        b_last = b_ref[i, c - 1:c, :]
        qe_ref[i] = q_ref[i] * jnp.exp2(b_ref[i]).astype(BF16)
        ke_ref[i] = k_ref[i] * jnp.exp2(b_last - b_ref[i]).astype(BF16)
        state_decay[i] = jnp.exp2(b_last)

    def level(i, lvl):
        if lvl == n_levels - 1:
            return
        if lvl == n_levels:
            ql, kl = q_ref[i], k_ref[i]
        elif levels[lvl] >= HG_VPU_MIN_HALF:
            s = levels[lvl]
            q_rows, k_rows = [], []
            for r0 in range(0, c, 2 * s):
                lo = slice(r0, r0 + s)
                up = slice(r0 + s, r0 + 2 * s)
                ref = b_ref[i, r0 + s - 1:r0 + s, :]
                zeros = jnp.zeros((s, HG_KDIM), BF16)
                q_rows += [zeros, q_ref[i, up, :] * jnp.exp2(b_ref[i, up, :] - ref).astype(BF16)]
                k_rows += [k_ref[i, lo, :] * jnp.exp2(ref - b_ref[i, lo, :]).astype(BF16), zeros]
            ql = jnp.concatenate(q_rows, axis=0)
            kl = jnp.concatenate(k_rows, axis=0)
        else:
            n_fine = 1 + sum(1 for s in levels[:lvl] if 1 < s < HG_VPU_MIN_HALF)
            ex = jnp.exp2(fine[i][n_fine * c:(n_fine + 1) * c]).astype(BF16)
            ql = q_ref[i] * ex
            kl = k_ref[i] * ex
        ql_ref[i, lvl] = ql
        klt_ref[i, lvl] = kl.T

    zero = jnp.zeros((HG_K, c), BF16)

    def head_pair(i, pair):
        psl = slice(2 * pair * HG_K, (2 * pair + 2) * HG_K)
        def block_diag(lvl):
            kt = klt_ref[i, lvl, psl, :]
            return jnp.concatenate(
                [jnp.concatenate([kt[:HG_K], zero], axis=1),
                 jnp.concatenate([zero, kt[HG_K:]], axis=1)], axis=0)

        att = _dot(ql_ref[i, 0, :, psl], block_diag(0))
        for lvl in range(1, n_levels - 1):
            att = att + _dot(ql_ref[i, lvl, :, psl], block_diag(lvl)) * mask_ref[lvl]
        kbd = block_diag(n_levels)
        for lvl in (n_levels - 1, n_levels):
            att = att + _dot(ql_ref[i, lvl, :, psl], kbd) * mask_ref[lvl]
        for sub in range(2):
            h = 2 * pair + sub
            sl = slice(h * HG_K, (h + 1) * HG_K)
            vh = v_ref[i, :, sl]
            s_t = state_ref[i, h]
            o = _dot_nt(qe_ref[i, :, sl], s_t.astype(BF16))
            o = o + _dot(att[:, sub * c:(sub + 1) * c].astype(BF16), vh)
            state_ref[i, h] = s_t * state_decay[i][:, sl] + _dot_tn(vh, ke_ref[i, :, sl])
            o_ref[i, :, sl] = (_rms(o, nw_ref[...]) * gate_ref[i, :, sl]).astype(o_ref.dtype)

    project(0)
    project(1)
    for i in seqs:
        cumsums(i)
    n_pairs = HG_HEADS // 2
    per_pair = -(-(n_levels + 1) // n_pairs)
    for lvl in range(n_levels + 1):
        level(0, lvl)
        if 2 + lvl < HG_PROJ_PIECES:
            project(2 + lvl)
    for i in seqs:
        for pair in range(n_pairs):
            head_pair(i, pair)
            if i + 1 < n_seq:
                for lvl in range(pair * per_pair, min((pair + 1) * per_pair, n_levels + 1)):
                    level(i + 1, lvl)


def _hgrn2(x, norm_mix_w, w_hg, lb_params, norm_w):
    n_seq, seqlen, d = x.shape
    c = HG_CHUNK
    n_chunks = seqlen // c
    n_levels = len(_hgrn2_levels(c))
    sums, masks = _hgrn2_tables(c)
    consts = (norm_mix_w, w_hg, lb_params, norm_w, sums, masks)
    x_spec, o_spec = _lookahead_specs(n_seq, n_chunks, c, d, HG_VDIM)
    wide = lambda dt, *lead: pltpu.VMEM((n_seq, *lead, c, HG_KDIM), dt)
    return pl.pallas_call(
        _hgrn2_kernel,
        grid=(n_chunks + 1,),
        in_specs=[x_spec] + [_const_spec(a.shape) for a in consts],
        out_specs=o_spec,
        out_shape=jax.ShapeDtypeStruct((n_seq, seqlen, HG_VDIM), BF16),
        scratch_shapes=[pltpu.VMEM((n_seq * c, HG_PROJ), F32),
                        pltpu.VMEM((n_seq, HG_HEADS, HG_V, HG_K), F32),
                        pltpu.VMEM((n_seq, 2 * c, HG_KDIM), BF16),
                        wide(BF16), wide(BF16), wide(F32),
                        wide(BF16, n_levels + 1),
                        pltpu.VMEM((n_seq, n_levels + 1, HG_KDIM, c), BF16),
                        wide(BF16), wide(BF16),
                        wide(BF16), wide(F32)],
        compiler_params=_cparams(("arbitrary",)),
        name="hgrn2",
    )(x, *consts)


def _attn_kernel(x_ref, ya_ref, ob_ref, wout_ref, nw_ref, wq_ref, km_ref, vm_ref, wo_ref,
                 o_ref, ox_ref):
    x1 = x_ref[...] + _dot(ya_ref[...], wout_ref[0:SSD_DIM, :]) \
        + _dot(ob_ref[...], wout_ref[SSD_DIM:, :])
    h = _rms(x1, nw_ref[...]).astype(BF16)
    qx = (_dot(h, wq_ref[...]) * (XA_HEAD_DIM ** -0.5)).astype(BF16)
    for hd in range(XA_HEADS):
        sl = slice(hd * XA_HEAD_DIM, (hd + 1) * XA_HEAD_DIM)
        sc = _dot_nt(qx[:, sl], km_ref[0, :, sl])
        p = jnp.exp(sc - jnp.max(sc, axis=-1, keepdims=True))
        denom = jnp.sum(p, axis=-1, keepdims=True)
        ox_ref[:, sl] = (_dot(p.astype(BF16), vm_ref[0, :, sl]) / denom).astype(BF16)
    o_ref[...] = x1 + _dot(ox_ref[...], wo_ref[...])


def _attn(x2d, ya, ob, w_out, norm_w, wq, km, vm, wo, seqlen):
    t, d = x2d.shape
    tm = min(ATTN_ROWS, seqlen)
    tpb = seqlen // tm
    mlen = km.shape[1]
    row = lambda n: pl.BlockSpec((tm, n), lambda i: (i, 0))
    mem = pl.BlockSpec((1, mlen, d), lambda i: (i // tpb, 0, 0))
    return pl.pallas_call(
        _attn_kernel,
        grid=(t // tm,),
        in_specs=[row(d), row(SSD_DIM), row(HG_VDIM), _const_spec(w_out.shape),
                  _const_spec(norm_w.shape), _const_spec(wq.shape), mem, mem,
                  _const_spec(wo.shape)],
        out_specs=row(d),
        out_shape=jax.ShapeDtypeStruct((t, d), F32),
        scratch_shapes=[pltpu.VMEM((tm, d), BF16)],
        compiler_params=_cparams(("arbitrary",)),
        name="attn",
    )(x2d, ya, ob, w_out, norm_w, wq, km, vm, wo)


def _ffn_kernel(x_ref, nw_ref, wg_ref, wu_ref, wd_ref, nf_ref, o_ref, acc_ref):
    x = x_ref[...]
    h = _rms(x, nw_ref[...]).astype(BF16)
    acc_ref[...] = x
    n_cols = wg_ref.shape[1]
    for c0 in range(0, n_cols, FFN_COLS):
        g = _dot(h, wg_ref[:, c0:c0 + FFN_COLS])
        u = _dot(h, wu_ref[:, c0:c0 + FFN_COLS])
        a = (_silu(g) * u).astype(BF16)
        acc_ref[...] += _dot(a, wd_ref[c0:c0 + FFN_COLS, :])
    o_ref[...] = _rms(acc_ref[...], nf_ref[...])


def _ffn(x2d, norm_w, wg, wu, wd, norm_final):
    t, d = x2d.shape
    tm = min(FFN_ROWS, t)
    row = pl.BlockSpec((tm, d), lambda i: (i, 0))
    return pl.pallas_call(
        _ffn_kernel,
        grid=(t // tm,),
        in_specs=[row, _const_spec(norm_w.shape), _const_spec(wg.shape), _const_spec(wu.shape),
                  _const_spec(wd.shape), _const_spec(norm_final.shape)],
        out_specs=row,
        out_shape=jax.ShapeDtypeStruct((t, d), F32),
        scratch_shapes=[pltpu.VMEM((tm, d), F32)],
        compiler_params=_cparams(("arbitrary",)),
        name="ffn",
    )(x2d, norm_w, wg, wu, wd, norm_final)


def _pad_lanes(v):
    return jnp.pad(v.astype(F32), (0, LANES - v.shape[0])).reshape(1, LANES)


def kernel(x, mem, norm_mix_w, w_in, conv_w, conv_b, dt_bias, a_log, d_skip, ssd_norm_w,
           hg_lower_bounds, hg_norm_w, w_out, norm_xa_w, norm_mem_w, xa_wq, xa_wkv, xa_wo,
           norm_ffn_w, ffn_w_gate, ffn_w_up, ffn_w_down, norm_final_w):
    bsz, seqlen, d = x.shape
    assert d == D_MODEL and norm_mix_w.shape[0] == 1, "single-layer block of width 1024"
    assert hg_lower_bounds.shape[0] == 2
    assert seqlen % SSD_CHUNK == 0 and seqlen % HG_CHUNK == 0
    t = bsz * seqlen
    x2d = x.reshape(t, d)
    row = lambda v: v.reshape(1, -1).astype(F32)

    s3 = SSD_DIM + SSD_CONV_CH + SSD_HEADS
    w = w_in[0].astype(BF16)
    w_ssd = jnp.pad(w[:, :s3], ((0, 0), (0, LANES - SSD_HEADS)))
    w_hg = w[:, s3:]

    expand = (np.arange(LANES)[:, None] == (np.arange(SSD_DIM)[None, :] // SSD_HEAD_DIM))
    expand = jnp.asarray(expand, BF16)
    d_skip_w = jnp.repeat(d_skip[0].astype(F32), SSD_HEAD_DIM).reshape(1, SSD_DIM)
    y_a = _ssd(x, row(norm_mix_w[0]), w_ssd, conv_w[0].astype(F32), row(conv_b[0]),
               _pad_lanes(dt_bias[0]), _pad_lanes(a_log[0]), d_skip_w, row(ssd_norm_w[0]),
               expand).reshape(t, SSD_DIM)
    o_b = _hgrn2(x, row(norm_mix_w[0]), w_hg, hg_lower_bounds.astype(F32),
                 row(hg_norm_w[0])).reshape(t, HG_VDIM)

    km, vm = _mem_kv(mem, row(norm_mem_w[0]), xa_wkv[0].astype(BF16))
    x2 = _attn(x2d, y_a, o_b, w_out[0].astype(BF16), row(norm_xa_w[0]), xa_wq[0].astype(BF16),
               km, vm, xa_wo[0].astype(BF16), seqlen)
    out = _ffn(x2, row(norm_ffn_w[0]), ffn_w_gate[0].astype(BF16), ffn_w_up[0].astype(BF16),
               ffn_w_down[0].astype(BF16), row(norm_final_w))
    return out.reshape(bsz, seqlen, d)
```

```python
import math

import numpy as np
import jax
import jax.numpy as jnp
from jax import lax
from jax.experimental import pallas as pl
from jax.experimental.pallas import tpu as pltpu

F32 = jnp.float32
BF16 = jnp.bfloat16
EPS = 1e-6
LOG2E = math.log2(math.e)

D_MODEL = 1024
SSD_HEADS = 16
SSD_HEAD_DIM = 64
SSD_DIM = SSD_HEADS * SSD_HEAD_DIM
SSD_GROUPS = 2
SSD_STATE = 128
SSD_CONV = 4
SSD_CHUNK = 128
SSD_CONV_CH = SSD_DIM + 2 * SSD_GROUPS * SSD_STATE
HG_HEADS = 8
HG_K = 128
HG_V = 128
HG_KDIM = HG_HEADS * HG_K
HG_VDIM = HG_HEADS * HG_V
XA_HEADS = 4
XA_HEAD_DIM = D_MODEL // XA_HEADS

LANES = 128
BF16_SUBLANES = 16
CONV_TAIL = BF16_SUBLANES
VMEM_LIMIT_BYTES = 56 * 1024 * 1024

HG_CHUNK = 128
ATTN_ROWS = 512
FFN_ROWS = 512
FFN_COLS = 256
W_SPLIT_ROWS = 128

SSD_PROJ = SSD_DIM + SSD_CONV_CH + LANES
SSD_PROJ_PIECE = 512
HG_PROJ = 2 * HG_KDIM + 2 * HG_VDIM


def _cparams(sem):
    return pltpu.CompilerParams(dimension_semantics=sem, vmem_limit_bytes=VMEM_LIMIT_BYTES)


def _const_spec(shape):
    zeros = (0,) * len(shape)
    return pl.BlockSpec(shape, lambda *_: zeros)


def _dot(a, b):
    return jnp.dot(a, b, preferred_element_type=F32)


def _dot_nt(a, b):
    return lax.dot_general(a, b, (((1,), (1,)), ((), ())), preferred_element_type=F32)


def _dot_tn(a, b):
    return lax.dot_general(a, b, (((0,), (0,)), ((), ())), preferred_element_type=F32)


def _sigmoid(x):
    return 1.0 / (1.0 + jnp.exp(-x))


def _silu(x):
    return x * _sigmoid(x)


def _rms(x, w):
    ms = jnp.mean(x * x, axis=-1, keepdims=True)
    return x * lax.rsqrt(ms + EPS) * w


def _split_hi_lo(x):
    hi = x.astype(BF16)
    lo = (x - hi.astype(F32)).astype(BF16)
    return hi, lo


def _mem_kv_kernel(mem_ref, nw_ref, wkv_ref, k_ref, v_ref):
    m = _rms(mem_ref[0], nw_ref[...]).astype(BF16)
    kv = _dot(m, wkv_ref[...])
    k_ref[0] = kv[:, :D_MODEL].astype(BF16)
    v_ref[0] = kv[:, D_MODEL:].astype(BF16)


def _mem_kv(mem, norm_w, wkv):
    bsz, mlen, d = mem.shape
    out = jax.ShapeDtypeStruct((bsz, mlen, d), BF16)
    return pl.pallas_call(
        _mem_kv_kernel,
        grid=(bsz,),
        in_specs=[pl.BlockSpec((1, mlen, d), lambda b: (b, 0, 0)),
                  _const_spec((1, d)), _const_spec((d, 2 * d))],
        out_specs=[pl.BlockSpec((1, mlen, d), lambda b: (b, 0, 0))] * 2,
        out_shape=[out, out],
        compiler_params=_cparams(("arbitrary",)),
        name="mem_kv",
    )(mem, norm_w, wkv)


def _w_split_kernel(w_ref, ssd_ref, hg_ref):
    w = w_ref[...]
    n_ssd = SSD_PROJ - LANES + SSD_HEADS
    pad = jnp.zeros((w.shape[0], SSD_PROJ - n_ssd), F32)
    ssd_ref[...] = jnp.concatenate([w[:, :n_ssd], pad], axis=1).astype(BF16)
    hg_ref[...] = w[:, n_ssd:].astype(BF16)


def _w_split(w):
    d, n_in = w.shape
    rows = W_SPLIT_ROWS
    return pl.pallas_call(
        _w_split_kernel,
        grid=(d // rows,),
        in_specs=[pl.BlockSpec((rows, n_in), lambda i: (i, 0))],
        out_specs=[pl.BlockSpec((rows, SSD_PROJ), lambda i: (i, 0)),
                   pl.BlockSpec((rows, HG_PROJ), lambda i: (i, 0))],
        out_shape=[jax.ShapeDtypeStruct((d, SSD_PROJ), BF16),
                   jax.ShapeDtypeStruct((d, HG_PROJ), BF16)],
        compiler_params=_cparams(("arbitrary",)),
        name="w_split",
    )(w)


def _ssd_kernel(x_ref, nmw_ref, w_ref, convw_ref, convb_ref, dtb_ref, alog_ref,
                dskip_ref, nw_ref, expand_ref, shift_ref, o_ref,
                proj_ref, ext_ref, state_ref, y_ref, zs_ref):
    n_seq, q = x_ref.shape[0], x_ref.shape[1]
    gw = SSD_DIM // SSD_GROUPS
    heads_per_group = SSD_HEADS // SSD_GROUPS
    step = pl.program_id(0)
    seqs = range(n_seq)

    @pl.when(step == 0)
    def _():
        proj_ref[...] = jnp.zeros_like(proj_ref)

    @pl.when(step <= 1)
    def _():
        ext_ref[:, 0:CONV_TAIL, :] = jnp.zeros((n_seq, CONV_TAIL, SSD_CONV_CH), BF16)
        state_ref[...] = jnp.zeros_like(state_ref)

    @pl.when(step > 1)
    def _():
        ext_ref[:, 0:CONV_TAIL, :] = ext_ref[:, q:q + CONV_TAIL, :]

    row = lax.broadcasted_iota(jnp.int32, (q, q), 0)
    colm = lax.broadcasted_iota(jnp.int32, (q, q), 1)
    causal = row >= colm
    tri = causal.astype(BF16)
    lane = lax.broadcasted_iota(jnp.int32, (q, LANES), 1)
    left = lane < SSD_HEAD_DIM

    h_next = _rms(x_ref[...].reshape(n_seq * q, D_MODEL), nmw_ref[...]).astype(BF16)
    n_pieces = -(-SSD_PROJ // SSD_PROJ_PIECE)

    def project(j):
        cols = slice(j * SSD_PROJ_PIECE, min((j + 1) * SSD_PROJ_PIECE, SSD_PROJ))
        proj_ref[:, cols] = _dot(h_next, w_ref[:, cols])

    xs, xdt, xw, b_all, c_all, acum, acum_t, dfs_w, cd_w = ([None] * n_seq for _ in range(9))
    u_cur, dtr = [None] * n_seq, [None] * n_seq
    for i in seqs:
        rows = slice(i * q, (i + 1) * q)
        zs_ref[i] = _silu(proj_ref[rows, 0:SSD_DIM])
        u_cur[i] = proj_ref[rows, SSD_DIM:SSD_DIM + SSD_CONV_CH].astype(BF16)
        ext_ref[i, CONV_TAIL:CONV_TAIL + q, :] = u_cur[i]
        dtr[i] = proj_ref[rows, SSD_DIM + SSD_CONV_CH:] + dtb_ref[...]

    for i in seqs:
        shifted = _dot(shift_ref[...], ext_ref[i])
        project(2 * i)
        acc = convb_ref[...] + convw_ref[SSD_CONV - 1:SSD_CONV, :] * u_cur[i].astype(F32)
        for k in range(SSD_CONV - 1):
            acc = acc + convw_ref[k:k + 1, :] * shifted[k * q:(k + 1) * q]
        xbc = _silu(acc)
        xs[i] = xbc[:, :SSD_DIM]
        b_all[i] = xbc[:, SSD_DIM:SSD_DIM + SSD_GROUPS * SSD_STATE]
        c_all[i] = xbc[:, SSD_DIM + SSD_GROUPS * SSD_STATE:]

        dt = jnp.maximum(dtr[i], 0.0) + jnp.log(1.0 + jnp.exp(-jnp.abs(dtr[i])))
        da = dt * (-jnp.exp(alog_ref[...]) * LOG2E)
        da_hi, da_lo = _split_hi_lo(da)
        da_lo2 = (da - da_hi.astype(F32) - da_lo.astype(F32)).astype(BF16)
        acum[i] = _dot(tri, da_hi) + _dot(tri, da_lo) + _dot(tri, da_lo2)
        project(2 * i + 1)
        acum_t[i] = acum[i].T
        a_last = acum[i][q - 1:q, :]
        dec_from_start = jnp.exp2(acum[i])
        dec_to_end = jnp.exp2(a_last - acum[i])
        chunk_decay = jnp.broadcast_to(jnp.exp2(a_last), (BF16_SUBLANES, LANES))

        stack = jnp.concatenate([dt, dec_from_start, dec_to_end], axis=0).astype(BF16)
        wide = _dot(stack, expand_ref[...])
        cd_hi, cd_lo = _split_hi_lo(chunk_decay)
        cd_w[i] = (_dot(cd_hi, expand_ref[...]) + _dot(cd_lo, expand_ref[...]))[0:1]
        dfs_w[i] = wide[q:2 * q]
        xdt[i] = xs[i] * wide[0:q]
        xw[i] = (xdt[i] * wide[2 * q:3 * q]).astype(BF16)

    for g in range(SSD_GROUPS):
        gsl = slice(g * gw, (g + 1) * gw)
        for i in seqs:
            b_g = b_all[i][:, g * SSD_STATE:(g + 1) * SSD_STATE].astype(BF16)
            c_g = c_all[i][:, g * SSD_STATE:(g + 1) * SSD_STATE].astype(BF16)
            cb = _dot_nt(c_g, b_g)
            s_in = state_ref[i, g]
            y_ref[i, :, gsl] = _dot(c_g, s_in.astype(BF16)) * dfs_w[i][:, gsl]
            state_ref[i, g] = s_in * cd_w[i][:, gsl] + _dot_tn(b_g, xw[i][:, gsl])
            for pair in range(heads_per_group // 2):
                h0 = g * heads_per_group + 2 * pair
                ms = []
                for h in (h0, h0 + 1):
                    seg = acum[i][:, h:h + 1] - acum_t[i][h:h + 1, :]
                    l_dec = jnp.exp2(jnp.where(causal, seg, -jnp.inf))
                    ms.append((cb * l_dec).astype(BF16))
                lhs = jnp.concatenate(ms, axis=1)
                psl = slice(h0 * SSD_HEAD_DIM, (h0 + 2) * SSD_HEAD_DIM)
                xp = xdt[i][:, psl]
                rhs = jnp.concatenate([jnp.where(left, xp, 0.0), jnp.where(left, 0.0, xp)],
                                      axis=0).astype(BF16)
                y_ref[i, :, psl] += _dot(lhs, rhs)

    for j in range(2 * n_seq, n_pieces):
        project(j)
    for i in seqs:
        yz = (y_ref[i] + dskip_ref[...] * xs[i]) * zs_ref[i]
        for g in range(SSD_GROUPS):
            gsl = slice(g * gw, (g + 1) * gw)
            o_ref[i, :, gsl] = _rms(yz[:, gsl], nw_ref[:, gsl]).astype(o_ref.dtype)


def _conv_shift_matrix(q):
    s = np.zeros(((SSD_CONV - 1) * q, CONV_TAIL + q), np.float32)
    for k in range(SSD_CONV - 1):
        for t in range(q):
            s[k * q + t, CONV_TAIL + t - (SSD_CONV - 1) + k] = 1.0
    return jnp.asarray(s, BF16)


def _lookahead_specs(n_seq, n_chunks, rows, width_in, width_out):
    x_spec = pl.BlockSpec((n_seq, rows, width_in),
                          lambda s: (0, jnp.minimum(s, n_chunks - 1), 0))
    o_spec = pl.BlockSpec((n_seq, rows, width_out), lambda s: (0, jnp.maximum(s - 1, 0), 0))
    return x_spec, o_spec


def _ssd(x, norm_mix_w, w_ssd, conv_w, conv_b, dt_bias, a_log, d_skip_w, norm_w, expand):
    n_seq, seqlen, d = x.shape
    q = SSD_CHUNK
    n_chunks = seqlen // q
    shift = _conv_shift_matrix(q)
    x_spec, o_spec = _lookahead_specs(n_seq, n_chunks, q, d, SSD_DIM)
    consts = (norm_mix_w, w_ssd, conv_w, conv_b, dt_bias, a_log, d_skip_w, norm_w, expand, shift)
    return pl.pallas_call(
        _ssd_kernel,
        grid=(n_chunks + 1,),
        in_specs=[x_spec] + [_const_spec(c.shape) for c in consts],
        out_specs=o_spec,
        out_shape=jax.ShapeDtypeStruct((n_seq, seqlen, SSD_DIM), BF16),
        scratch_shapes=[pltpu.VMEM((n_seq * q, SSD_PROJ), F32),
                        pltpu.VMEM((n_seq, CONV_TAIL + q, SSD_CONV_CH), BF16),
                        pltpu.VMEM((n_seq, SSD_GROUPS, SSD_STATE, SSD_DIM // SSD_GROUPS), F32),
                        pltpu.VMEM((n_seq, q, SSD_DIM), F32),
                        pltpu.VMEM((n_seq, q, SSD_DIM), F32)],
        compiler_params=_cparams(("arbitrary",)),
        name="ssd",
    )(x, *consts)


HG_PROJ_PIECES = 8
HG_VPU_MIN_HALF = 8


def _hgrn2_levels(c):
    levels = []
    s = c // 2
    while s >= 1:
        levels.append(s)
        s //= 2
    return levels


def _hgrn2_tables(c):
    idx = np.arange(c)
    blocks = [np.tril(np.ones((c, c), np.float32))]
    masks = []
    for s in _hgrn2_levels(c):
        mid = (idx // (2 * s)) * (2 * s) + s
        upper = idx >= mid
        if 1 < s < HG_VPU_MIN_HALF:
            a = np.zeros((c, c), np.float32)
            for i in range(c):
                if upper[i]:
                    a[i, mid[i]:i + 1] = 1.0
                else:
                    a[i, i + 1:mid[i]] = 1.0
            blocks.append(a)
        same = (idx[:, None] // (2 * s)) == (idx[None, :] // (2 * s))
        masks.append((same & upper[:, None] & (~upper)[None, :]).astype(np.float32))
    masks.append(np.eye(c, dtype=np.float32))
    sums = np.concatenate(blocks, axis=0)
    sums2 = np.concatenate([sums, sums], axis=1)
    masks = np.stack(masks)
    masks2 = np.concatenate([masks, masks], axis=2)
    return jnp.asarray(sums2, BF16), jnp.asarray(masks2, F32)


def _hgrn2_kernel(x_ref, nmw_ref, w_ref, lbp_ref, nw_ref, sums_ref, mask_ref, o_ref,
                  proj_ref, state_ref, g2_ref, q_ref, k_ref, b_ref, ql_ref, klt_ref, qe_ref,
                  ke_ref, v_ref, gate_ref):
    n_seq, c = x_ref.shape[0], x_ref.shape[1]
    levels = _hgrn2_levels(c)
    n_levels = len(levels)
    step = pl.program_id(0)
    seqs = range(n_seq)

    @pl.when(step == 0)
    def _():
        proj_ref[...] = jnp.zeros_like(proj_ref)

    @pl.when(step <= 1)
    def _():
        state_ref[...] = jnp.zeros_like(state_ref)

    lbp = lbp_ref[...]
    e = jnp.exp(lbp - jnp.max(lbp, axis=0, keepdims=True))
    lb = e[0:1, :] / jnp.sum(e, axis=0, keepdims=True)

    for i in seqs:
        rows = slice(i * c, (i + 1) * c)
        q_ref[i] = _silu(proj_ref[rows, 0:HG_KDIM])
        fg = lb + (1.0 - lb) * _sigmoid(proj_ref[rows, HG_KDIM:2 * HG_KDIM])
        v_ref[i] = proj_ref[rows, 2 * HG_KDIM:2 * HG_KDIM + HG_VDIM].astype(BF16)
        gate_ref[i] = _silu(proj_ref[rows, 2 * HG_KDIM + HG_VDIM:])
        g_hi, g_lo = _split_hi_lo(jnp.log2(fg))
        g2_ref[i, 0:c, :] = g_hi
        g2_ref[i, c:2 * c, :] = g_lo
        k_ref[i] = 1.0 - fg
        ql_ref[i, n_levels - 1] = (q_ref[i] * fg).astype(BF16)

    h_next = _rms(x_ref[...].reshape(n_seq * c, D_MODEL), nmw_ref[...]).astype(BF16)
    piece = HG_PROJ // HG_PROJ_PIECES

    def project(j):
        cols = slice(j * piece, (j + 1) * piece)
        proj_ref[:, cols] = _dot(h_next, w_ref[:, cols])

    fine, state_decay = [None] * n_seq, [None] * n_seq

    def cumsums(i):
        fine[i] = _dot(sums_ref[...], g2_ref[i])
        b_ref[i] = fine[i][0:c]
        b_last = b_ref[i, c - 1:c, :]
        qe_ref[i] = (q_ref[i] * jnp.exp2(b_ref[i])).astype(BF16)
        ke_ref[i] = (k_ref[i] * jnp.exp2(b_last - b_ref[i])).astype(BF16)
        state_decay[i] = jnp.exp2(b_last)

    def level(i, lvl):
        if lvl == n_levels - 1:
            return
        if lvl == n_levels:
            ql, kl = q_ref[i], k_ref[i]
        elif levels[lvl] >= HG_VPU_MIN_HALF:
            s = levels[lvl]
            q_rows, k_rows = [], []
            for r0 in range(0, c, 2 * s):
                lo = slice(r0, r0 + s)
                up = slice(r0 + s, r0 + 2 * s)
                ref = b_ref[i, r0 + s - 1:r0 + s, :]
                zeros = jnp.zeros((s, HG_KDIM), F32)
                q_rows += [zeros, q_ref[i, up, :] * jnp.exp2(b_ref[i, up, :] - ref)]
                k_rows += [k_ref[i, lo, :] * jnp.exp2(ref - b_ref[i, lo, :]), zeros]
            ql = jnp.concatenate(q_rows, axis=0)
            kl = jnp.concatenate(k_rows, axis=0)
        else:
            n_fine = 1 + sum(1 for s in levels[:lvl] if 1 < s < HG_VPU_MIN_HALF)
            ex = jnp.exp2(fine[i][n_fine * c:(n_fine + 1) * c])
            ql = q_ref[i] * ex
            kl = k_ref[i] * ex
        ql_ref[i, lvl] = ql.astype(BF16)
        klt_ref[i, lvl] = kl.T

    zero = jnp.zeros((HG_K, c), BF16)

    def head_pair(i, pair):
        psl = slice(2 * pair * HG_K, (2 * pair + 2) * HG_K)
        def block_diag(lvl):
            kt = klt_ref[i, lvl, psl, :].astype(BF16)
            return jnp.concatenate(
                [jnp.concatenate([kt[:HG_K], zero], axis=1),
                 jnp.concatenate([zero, kt[HG_K:]], axis=1)], axis=0)

        att = _dot(ql_ref[i, 0, :, psl], block_diag(0))
        for lvl in range(1, n_levels - 1):
            att = att + _dot(ql_ref[i, lvl, :, psl], block_diag(lvl)) * mask_ref[lvl]
        kbd = block_diag(n_levels)
        for lvl in (n_levels - 1, n_levels):
            att = att + _dot(ql_ref[i, lvl, :, psl], kbd) * mask_ref[lvl]
        for sub in range(2):
            h = 2 * pair + sub
            sl = slice(h * HG_K, (h + 1) * HG_K)
            vh = v_ref[i, :, sl]
            s_t = state_ref[i, h]
            o = _dot_nt(qe_ref[i, :, sl], s_t.astype(BF16))
            o = o + _dot(att[:, sub * c:(sub + 1) * c].astype(BF16), vh)
            state_ref[i, h] = s_t * state_decay[i][:, sl] + _dot_tn(vh, ke_ref[i, :, sl])
            o_ref[i, :, sl] = (_rms(o, nw_ref[...]) * gate_ref[i, :, sl]).astype(o_ref.dtype)

    project(0)
    project(1)
    for i in seqs:
        cumsums(i)
    n_pairs = HG_HEADS // 2
    per_pair = -(-(n_levels + 1) // n_pairs)
    for lvl in range(n_levels + 1):
        level(0, lvl)
        if 2 + lvl < HG_PROJ_PIECES:
            project(2 + lvl)
    for i in seqs:
        for pair in range(n_pairs):
            head_pair(i, pair)
            if i + 1 < n_seq:
                for lvl in range(pair * per_pair, min((pair + 1) * per_pair, n_levels + 1)):
                    level(i + 1, lvl)


def _hgrn2(x, norm_mix_w, w_hg, lb_params, norm_w):
    n_seq, seqlen, d = x.shape
    c = HG_CHUNK
    n_chunks = seqlen // c
    n_levels = len(_hgrn2_levels(c))
    sums, masks = _hgrn2_tables(c)
    consts = (norm_mix_w, w_hg, lb_params, norm_w, sums, masks)
    x_spec, o_spec = _lookahead_specs(n_seq, n_chunks, c, d, HG_VDIM)
    wide = lambda dt, *lead: pltpu.VMEM((n_seq, *lead, c, HG_KDIM), dt)
    return pl.pallas_call(
        _hgrn2_kernel,
        grid=(n_chunks + 1,),
        in_specs=[x_spec] + [_const_spec(a.shape) for a in consts],
        out_specs=o_spec,
        out_shape=jax.ShapeDtypeStruct((n_seq, seqlen, HG_VDIM), BF16),
        scratch_shapes=[pltpu.VMEM((n_seq * c, HG_PROJ), F32),
                        pltpu.VMEM((n_seq, HG_HEADS, HG_V, HG_K), F32),
                        pltpu.VMEM((n_seq, 2 * c, HG_KDIM), BF16),
                        wide(F32), wide(F32), wide(F32),
                        wide(BF16, n_levels + 1),
                        pltpu.VMEM((n_seq, n_levels + 1, HG_KDIM, c), F32),
                        wide(BF16), wide(BF16),
                        wide(BF16), wide(F32)],
        compiler_params=_cparams(("arbitrary",)),
        name="hgrn2",
    )(x, *consts)


def _attn_kernel(x_ref, ya_ref, ob_ref, wout_ref, nw_ref, wq_ref, km_ref, vm_ref, wo_ref,
                 o_ref, ox_ref):
    x1 = x_ref[...] + _dot(ya_ref[...], wout_ref[0:SSD_DIM, :]) \
        + _dot(ob_ref[...], wout_ref[SSD_DIM:, :])
    h = _rms(x1, nw_ref[...]).astype(BF16)
    qx = (_dot(h, wq_ref[...]) * (XA_HEAD_DIM ** -0.5)).astype(BF16)
    for hd in range(XA_HEADS):
        sl = slice(hd * XA_HEAD_DIM, (hd + 1) * XA_HEAD_DIM)
        sc = _dot_nt(qx[:, sl], km_ref[0, :, sl])
        p = jnp.exp(sc - jnp.max(sc, axis=-1, keepdims=True))
        denom = jnp.sum(p, axis=-1, keepdims=True)
        ox_ref[:, sl] = (_dot(p.astype(BF16), vm_ref[0, :, sl]) / denom).astype(BF16)
    o_ref[...] = x1 + _dot(ox_ref[...], wo_ref[...])


def _attn(x2d, ya, ob, w_out, norm_w, wq, km, vm, wo, seqlen):
    t, d = x2d.shape
    tm = min(ATTN_ROWS, seqlen)
    tpb = seqlen // tm
    mlen = km.shape[1]
    row = lambda n: pl.BlockSpec((tm, n), lambda i: (i, 0))
    mem = pl.BlockSpec((1, mlen, d), lambda i: (i // tpb, 0, 0))
    return pl.pallas_call(
        _attn_kernel,
        grid=(t // tm,),
        in_specs=[row(d), row(SSD_DIM), row(HG_VDIM), _const_spec(w_out.shape),
                  _const_spec(norm_w.shape), _const_spec(wq.shape), mem, mem,
                  _const_spec(wo.shape)],
        out_specs=row(d),
        out_shape=jax.ShapeDtypeStruct((t, d), F32),
        scratch_shapes=[pltpu.VMEM((tm, d), BF16)],
        compiler_params=_cparams(("arbitrary",)),
        name="attn",
    )(x2d, ya, ob, w_out, norm_w, wq, km, vm, wo)


def _ffn_kernel(x_ref, nw_ref, wg_ref, wu_ref, wd_ref, nf_ref, o_ref, acc_ref):
    x = x_ref[...]
    h = _rms(x, nw_ref[...]).astype(BF16)
    acc_ref[...] = x
    n_cols = wg_ref.shape[1]
    for c0 in range(0, n_cols, FFN_COLS):
        g = _dot(h, wg_ref[:, c0:c0 + FFN_COLS])
        u = _dot(h, wu_ref[:, c0:c0 + FFN_COLS])
        a = (_silu(g) * u).astype(BF16)
        acc_ref[...] += _dot(a, wd_ref[c0:c0 + FFN_COLS, :])
    o_ref[...] = _rms(acc_ref[...], nf_ref[...])


def _ffn(x2d, norm_w, wg, wu, wd, norm_final):
    t, d = x2d.shape
    tm = min(FFN_ROWS, t)
    row = pl.BlockSpec((tm, d), lambda i: (i, 0))
    return pl.pallas_call(
        _ffn_kernel,
        grid=(t // tm,),
        in_specs=[row, _const_spec(norm_w.shape), _const_spec(wg.shape), _const_spec(wu.shape),
                  _const_spec(wd.shape), _const_spec(norm_final.shape)],
        out_specs=row,
        out_shape=jax.ShapeDtypeStruct((t, d), F32),
        scratch_shapes=[pltpu.VMEM((tm, d), F32)],
        compiler_params=_cparams(("arbitrary",)),
        name="ffn",
    )(x2d, norm_w, wg, wu, wd, norm_final)


def _pad_lanes(v):
    return jnp.pad(v.astype(F32), (0, LANES - v.shape[0])).reshape(1, LANES)


def kernel(x, mem, norm_mix_w, w_in, conv_w, conv_b, dt_bias, a_log, d_skip, ssd_norm_w,
           hg_lower_bounds, hg_norm_w, w_out, norm_xa_w, norm_mem_w, xa_wq, xa_wkv, xa_wo,
           norm_ffn_w, ffn_w_gate, ffn_w_up, ffn_w_down, norm_final_w):
    bsz, seqlen, d = x.shape
    assert d == D_MODEL and norm_mix_w.shape[0] == 1, "single-layer block of width 1024"
    assert hg_lower_bounds.shape[0] == 2
    assert seqlen % SSD_CHUNK == 0 and seqlen % HG_CHUNK == 0
    t = bsz * seqlen
    x2d = x.reshape(t, d)
    row = lambda v: v.reshape(1, -1).astype(F32)

    w_ssd, w_hg = _w_split(w_in[0])

    expand = (np.arange(LANES)[:, None] == (np.arange(SSD_DIM)[None, :] // SSD_HEAD_DIM))
    expand = jnp.asarray(expand, BF16)
    d_skip_w = jnp.repeat(d_skip[0].astype(F32), SSD_HEAD_DIM).reshape(1, SSD_DIM)
    y_a = _ssd(x, row(norm_mix_w[0]), w_ssd, conv_w[0].astype(F32), row(conv_b[0]),
               _pad_lanes(dt_bias[0]), _pad_lanes(a_log[0]), d_skip_w, row(ssd_norm_w[0]),
               expand).reshape(t, SSD_DIM)
    o_b = _hgrn2(x, row(norm_mix_w[0]), w_hg, hg_lower_bounds.astype(F32),
                 row(hg_norm_w[0])).reshape(t, HG_VDIM)

    km, vm = _mem_kv(mem, row(norm_mem_w[0]), xa_wkv[0].astype(BF16))
    x2 = _attn(x2d, y_a, o_b, w_out[0].astype(BF16), row(norm_xa_w[0]), xa_wq[0].astype(BF16),
               km, vm, xa_wo[0].astype(BF16), seqlen)
    out = _ffn(x2, row(norm_ffn_w[0]), ffn_w_gate[0].astype(BF16), ffn_w_up[0].astype(BF16),
               ffn_w_down[0].astype(BF16), row(norm_final_w))
    return out.reshape(bsz, seqlen, d)
```

```python
import math

import numpy as np
import jax
import jax.numpy as jnp
from jax import lax
from jax.experimental import pallas as pl
from jax.experimental.pallas import tpu as pltpu

F32 = jnp.float32
BF16 = jnp.bfloat16
EPS = 1e-6
LOG2E = math.log2(math.e)

D_MODEL = 1024
SSD_HEADS = 16
SSD_HEAD_DIM = 64
SSD_DIM = SSD_HEADS * SSD_HEAD_DIM
SSD_GROUPS = 2
SSD_STATE = 128
SSD_CONV = 4
SSD_CHUNK = 128
SSD_CONV_CH = SSD_DIM + 2 * SSD_GROUPS * SSD_STATE
HG_HEADS = 8
HG_K = 128
HG_V = 128
HG_KDIM = HG_HEADS * HG_K
HG_VDIM = HG_HEADS * HG_V
XA_HEADS = 4
XA_HEAD_DIM = D_MODEL // XA_HEADS

LANES = 128
BF16_SUBLANES = 16
CONV_TAIL = BF16_SUBLANES
VMEM_LIMIT_BYTES = 56 * 1024 * 1024

HG_CHUNK = 128
ATTN_ROWS = 512
FFN_ROWS = 512
FFN_COLS = 256
W_SPLIT_ROWS = 128

SSD_PROJ = SSD_DIM + SSD_CONV_CH + LANES
SSD_PROJ_PIECE = 512
HG_PROJ = 2 * HG_KDIM + 2 * HG_VDIM


def _cparams(sem):
    return pltpu.CompilerParams(dimension_semantics=sem, vmem_limit_bytes=VMEM_LIMIT_BYTES)


def _const_spec(shape):
    zeros = (0,) * len(shape)
    return pl.BlockSpec(shape, lambda *_: zeros)


def _dot(a, b):
    return jnp.dot(a, b, preferred_element_type=F32)


def _dot_nt(a, b):
    return lax.dot_general(a, b, (((1,), (1,)), ((), ())), preferred_element_type=F32)


def _dot_tn(a, b):
    return lax.dot_general(a, b, (((0,), (0,)), ((), ())), preferred_element_type=F32)


def _sigmoid(x):
    return 1.0 / (1.0 + jnp.exp(-x))


def _silu(x):
    return x * _sigmoid(x)


def _rms(x, w):
    ms = jnp.mean(x * x, axis=-1, keepdims=True)
    return x * lax.rsqrt(ms + EPS) * w


def _split_hi_lo(x):
    hi = x.astype(BF16)
    lo = (x - hi.astype(F32)).astype(BF16)
    return hi, lo


def _mem_kv_kernel(mem_ref, nw_ref, wkv_ref, k_ref, v_ref):
    m = _rms(mem_ref[0], nw_ref[...]).astype(BF16)
    kv = _dot(m, wkv_ref[...])
    k_ref[0] = kv[:, :D_MODEL].astype(BF16)
    v_ref[0] = kv[:, D_MODEL:].astype(BF16)


def _mem_kv(mem, norm_w, wkv):
    bsz, mlen, d = mem.shape
    out = jax.ShapeDtypeStruct((bsz, mlen, d), BF16)
    return pl.pallas_call(
        _mem_kv_kernel,
        grid=(bsz,),
        in_specs=[pl.BlockSpec((1, mlen, d), lambda b: (b, 0, 0)),
                  _const_spec((1, d)), _const_spec((d, 2 * d))],
        out_specs=[pl.BlockSpec((1, mlen, d), lambda b: (b, 0, 0))] * 2,
        out_shape=[out, out],
        compiler_params=_cparams(("arbitrary",)),
        name="mem_kv",
    )(mem, norm_w, wkv)


def _w_split_kernel(w_ref, ssd_ref, hg_ref):
    w = w_ref[0]
    n_ssd = SSD_PROJ - LANES + SSD_HEADS
    pad = jnp.zeros((w.shape[0], SSD_PROJ - n_ssd), F32)
    ssd_ref[...] = jnp.concatenate([w[:, :n_ssd], pad], axis=1).astype(BF16)
    hg_ref[...] = w[:, n_ssd:].astype(BF16)


def _w_split(w):
    _, d, n_in = w.shape
    rows = W_SPLIT_ROWS
    return pl.pallas_call(
        _w_split_kernel,
        grid=(d // rows,),
        in_specs=[pl.BlockSpec((1, rows, n_in), lambda i: (0, i, 0))],
        out_specs=[pl.BlockSpec((rows, SSD_PROJ), lambda i: (i, 0)),
                   pl.BlockSpec((rows, HG_PROJ), lambda i: (i, 0))],
        out_shape=[jax.ShapeDtypeStruct((d, SSD_PROJ), BF16),
                   jax.ShapeDtypeStruct((d, HG_PROJ), BF16)],
        compiler_params=_cparams(("arbitrary",)),
        name="w_split",
    )(w)


def _ssd_kernel(x_ref, nmw_ref, w_ref, convw_ref, convb_ref, dtb_ref, alog_ref,
                dskip_ref, nw_ref, expand_ref, shift_ref, o_ref,
                proj_ref, ext_ref, state_ref, y_ref, zs_ref):
    n_seq, q = x_ref.shape[0], x_ref.shape[1]
    gw = SSD_DIM // SSD_GROUPS
    heads_per_group = SSD_HEADS // SSD_GROUPS
    step = pl.program_id(0)
    seqs = range(n_seq)

    @pl.when(step == 0)
    def _():
        proj_ref[...] = jnp.zeros_like(proj_ref)

    @pl.when(step <= 1)
    def _():
        ext_ref[:, 0:CONV_TAIL, :] = jnp.zeros((n_seq, CONV_TAIL, SSD_CONV_CH), BF16)
        state_ref[...] = jnp.zeros_like(state_ref)

    @pl.when(step > 1)
    def _():
        ext_ref[:, 0:CONV_TAIL, :] = ext_ref[:, q:q + CONV_TAIL, :]

    row = lax.broadcasted_iota(jnp.int32, (q, q), 0)
    colm = lax.broadcasted_iota(jnp.int32, (q, q), 1)
    causal = row >= colm
    tri = causal.astype(BF16)
    lane = lax.broadcasted_iota(jnp.int32, (q, LANES), 1)
    left = lane < SSD_HEAD_DIM

    h_next = _rms(x_ref[...].reshape(n_seq * q, D_MODEL), nmw_ref[...]).astype(BF16)
    n_pieces = -(-SSD_PROJ // SSD_PROJ_PIECE)

    def project(j):
        cols = slice(j * SSD_PROJ_PIECE, min((j + 1) * SSD_PROJ_PIECE, SSD_PROJ))
        proj_ref[:, cols] = _dot(h_next, w_ref[:, cols])

    xs, xdt, xw, b_all, c_all, acum, acum_t, dfs_w, cd_w = ([None] * n_seq for _ in range(9))
    u_cur, dtr = [None] * n_seq, [None] * n_seq
    for i in seqs:
        rows = slice(i * q, (i + 1) * q)
        zs_ref[i] = _silu(proj_ref[rows, 0:SSD_DIM])
        u_cur[i] = proj_ref[rows, SSD_DIM:SSD_DIM + SSD_CONV_CH].astype(BF16)
        ext_ref[i, CONV_TAIL:CONV_TAIL + q, :] = u_cur[i]
        dtr[i] = proj_ref[rows, SSD_DIM + SSD_CONV_CH:] + dtb_ref[...]

    for i in seqs:
        shifted = _dot(shift_ref[...], ext_ref[i])
        project(2 * i)
        acc = convb_ref[...] + convw_ref[SSD_CONV - 1:SSD_CONV, :] * u_cur[i].astype(F32)
        for k in range(SSD_CONV - 1):
            acc = acc + convw_ref[k:k + 1, :] * shifted[k * q:(k + 1) * q]
        xbc = _silu(acc)
        xs[i] = xbc[:, :SSD_DIM]
        b_all[i] = xbc[:, SSD_DIM:SSD_DIM + SSD_GROUPS * SSD_STATE]
        c_all[i] = xbc[:, SSD_DIM + SSD_GROUPS * SSD_STATE:]

        dt = jnp.maximum(dtr[i], 0.0) + jnp.log(1.0 + jnp.exp(-jnp.abs(dtr[i])))
        da = dt * (-jnp.exp(alog_ref[...]) * LOG2E)
        da_hi, da_lo = _split_hi_lo(da)
        da_lo2 = (da - da_hi.astype(F32) - da_lo.astype(F32)).astype(BF16)
        acum[i] = _dot(tri, da_hi) + _dot(tri, da_lo) + _dot(tri, da_lo2)
        project(2 * i + 1)
        acum_t[i] = acum[i].T
        a_last = acum[i][q - 1:q, :]
        dec_from_start = jnp.exp2(acum[i])
        dec_to_end = jnp.exp2(a_last - acum[i])
        chunk_decay = jnp.broadcast_to(jnp.exp2(a_last), (BF16_SUBLANES, LANES))

        stack = jnp.concatenate([dt, dec_from_start, dec_to_end], axis=0).astype(BF16)
        wide = _dot(stack, expand_ref[...])
        cd_hi, cd_lo = _split_hi_lo(chunk_decay)
        cd_w[i] = (_dot(cd_hi, expand_ref[...]) + _dot(cd_lo, expand_ref[...]))[0:1]
        dfs_w[i] = wide[q:2 * q]
        xdt[i] = xs[i] * wide[0:q]
        xw[i] = (xdt[i] * wide[2 * q:3 * q]).astype(BF16)

    for g in range(SSD_GROUPS):
        gsl = slice(g * gw, (g + 1) * gw)
        for i in seqs:
            b_g = b_all[i][:, g * SSD_STATE:(g + 1) * SSD_STATE].astype(BF16)
            c_g = c_all[i][:, g * SSD_STATE:(g + 1) * SSD_STATE].astype(BF16)
            cb = _dot_nt(c_g, b_g)
            s_in = state_ref[i, g]
            y_ref[i, :, gsl] = _dot(c_g, s_in.astype(BF16)) * dfs_w[i][:, gsl]
            state_ref[i, g] = s_in * cd_w[i][:, gsl] + _dot_tn(b_g, xw[i][:, gsl])
            for pair in range(heads_per_group // 2):
                h0 = g * heads_per_group + 2 * pair
                ms = []
                for h in (h0, h0 + 1):
                    seg = acum[i][:, h:h + 1] - acum_t[i][h:h + 1, :]
                    l_dec = jnp.exp2(jnp.where(causal, seg, -jnp.inf))
                    ms.append((cb * l_dec).astype(BF16))
                lhs = jnp.concatenate(ms, axis=1)
                psl = slice(h0 * SSD_HEAD_DIM, (h0 + 2) * SSD_HEAD_DIM)
                xp = xdt[i][:, psl]
                rhs = jnp.concatenate([jnp.where(left, xp, 0.0), jnp.where(left, 0.0, xp)],
                                      axis=0).astype(BF16)
                y_ref[i, :, psl] += _dot(lhs, rhs)

    for j in range(2 * n_seq, n_pieces):
        project(j)
    for i in seqs:
        yz = (y_ref[i] + dskip_ref[...] * xs[i]) * zs_ref[i]
        for g in range(SSD_GROUPS):
            gsl = slice(g * gw, (g + 1) * gw)
            o_ref[i, :, gsl] = _rms(yz[:, gsl], nw_ref[:, gsl]).astype(o_ref.dtype)


def _conv_shift_matrix(q):
    s = np.zeros(((SSD_CONV - 1) * q, CONV_TAIL + q), np.float32)
    for k in range(SSD_CONV - 1):
        for t in range(q):
            s[k * q + t, CONV_TAIL + t - (SSD_CONV - 1) + k] = 1.0
    return jnp.asarray(s, BF16)


def _lookahead_specs(n_seq, n_chunks, rows, width_in, width_out):
    x_spec = pl.BlockSpec((n_seq, rows, width_in),
                          lambda s: (0, jnp.minimum(s, n_chunks - 1), 0))
    o_spec = pl.BlockSpec((n_seq, rows, width_out), lambda s: (0, jnp.maximum(s - 1, 0), 0))
    return x_spec, o_spec


def _ssd(x, norm_mix_w, w_ssd, conv_w, conv_b, dt_bias, a_log, d_skip_w, norm_w, expand):
    n_seq, seqlen, d = x.shape
    q = SSD_CHUNK
    n_chunks = seqlen // q
    shift = _conv_shift_matrix(q)
    x_spec, o_spec = _lookahead_specs(n_seq, n_chunks, q, d, SSD_DIM)
    consts = (norm_mix_w, w_ssd, conv_w, conv_b, dt_bias, a_log, d_skip_w, norm_w, expand, shift)
    return pl.pallas_call(
        _ssd_kernel,
        grid=(n_chunks + 1,),
        in_specs=[x_spec] + [_const_spec(c.shape) for c in consts],
        out_specs=o_spec,
        out_shape=jax.ShapeDtypeStruct((n_seq, seqlen, SSD_DIM), BF16),
        scratch_shapes=[pltpu.VMEM((n_seq * q, SSD_PROJ), F32),
                        pltpu.VMEM((n_seq, CONV_TAIL + q, SSD_CONV_CH), BF16),
                        pltpu.VMEM((n_seq, SSD_GROUPS, SSD_STATE, SSD_DIM // SSD_GROUPS), F32),
                        pltpu.VMEM((n_seq, q, SSD_DIM), F32),
                        pltpu.VMEM((n_seq, q, SSD_DIM), F32)],
        compiler_params=_cparams(("arbitrary",)),
        name="ssd",
    )(x, *consts)


HG_PROJ_PIECES = 8
HG_VPU_MIN_HALF = 8


def _hgrn2_levels(c):
    levels = []
    s = c // 2
    while s >= 1:
        levels.append(s)
        s //= 2
    return levels


def _hgrn2_tables(c):
    idx = np.arange(c)
    blocks = [np.tril(np.ones((c, c), np.float32))]
    masks = []
    for s in _hgrn2_levels(c):
        mid = (idx // (2 * s)) * (2 * s) + s
        upper = idx >= mid
        if 1 < s < HG_VPU_MIN_HALF:
            a = np.zeros((c, c), np.float32)
            for i in range(c):
                if upper[i]:
                    a[i, mid[i]:i + 1] = 1.0
                else:
                    a[i, i + 1:mid[i]] = 1.0
            blocks.append(a)
        same = (idx[:, None] // (2 * s)) == (idx[None, :] // (2 * s))
        masks.append((same & upper[:, None] & (~upper)[None, :]).astype(np.float32))
    masks.append(np.eye(c, dtype=np.float32))
    sums = np.concatenate(blocks, axis=0)
    sums2 = np.concatenate([sums, sums], axis=1)
    masks = np.stack(masks)
    masks2 = np.concatenate([masks, masks], axis=2)
    return jnp.asarray(sums2, BF16), jnp.asarray(masks2, F32)


def _hgrn2_kernel(x_ref, nmw_ref, w_ref, lbp_ref, nw_ref, sums_ref, mask_ref, o_ref,
                  proj_ref, state_ref, g2_ref, q_ref, k_ref, b_ref, ql_ref, klt_ref, qe_ref,
                  ke_ref, v_ref, gate_ref):
    n_seq, c = x_ref.shape[0], x_ref.shape[1]
    levels = _hgrn2_levels(c)
    n_levels = len(levels)
    step = pl.program_id(0)
    seqs = range(n_seq)

    @pl.when(step == 0)
    def _():
        proj_ref[...] = jnp.zeros_like(proj_ref)

    @pl.when(step <= 1)
    def _():
        state_ref[...] = jnp.zeros_like(state_ref)

    lbp = lbp_ref[...]
    e = jnp.exp(lbp - jnp.max(lbp, axis=0, keepdims=True))
    lb = e[0:1, :] / jnp.sum(e, axis=0, keepdims=True)

    for i in seqs:
        rows = slice(i * c, (i + 1) * c)
        q_ref[i] = _silu(proj_ref[rows, 0:HG_KDIM])
        fg = lb + (1.0 - lb) * _sigmoid(proj_ref[rows, HG_KDIM:2 * HG_KDIM])
        v_ref[i] = proj_ref[rows, 2 * HG_KDIM:2 * HG_KDIM + HG_VDIM].astype(BF16)
        gate_ref[i] = _silu(proj_ref[rows, 2 * HG_KDIM + HG_VDIM:])
        g_hi, g_lo = _split_hi_lo(jnp.log2(fg))
        g2_ref[i, 0:c, :] = g_hi
        g2_ref[i, c:2 * c, :] = g_lo
        k_ref[i] = 1.0 - fg
        ql_ref[i, n_levels - 1] = (q_ref[i] * fg).astype(BF16)

    h_next = _rms(x_ref[...].reshape(n_seq * c, D_MODEL), nmw_ref[...]).astype(BF16)
    piece = HG_PROJ // HG_PROJ_PIECES

    def project(j):
        cols = slice(j * piece, (j + 1) * piece)
        proj_ref[:, cols] = _dot(h_next, w_ref[:, cols])

    fine, state_decay = [None] * n_seq, [None] * n_seq

    def cumsums(i):
        fine[i] = _dot(sums_ref[...], g2_ref[i])
        b_ref[i] = fine[i][0:c]
        b_last = b_ref[i, c - 1:c, :]
        qe_ref[i] = (q_ref[i] * jnp.exp2(b_ref[i])).astype(BF16)
        ke_ref[i] = (k_ref[i] * jnp.exp2(b_last - b_ref[i])).astype(BF16)
        state_decay[i] = jnp.exp2(b_last)

    def level(i, lvl):
        if lvl == n_levels - 1:
            return
        if lvl == n_levels:
            ql, kl = q_ref[i], k_ref[i]
        elif levels[lvl] >= HG_VPU_MIN_HALF:
            s = levels[lvl]
            q_rows, k_rows = [], []
            for r0 in range(0, c, 2 * s):
                lo = slice(r0, r0 + s)
                up = slice(r0 + s, r0 + 2 * s)
                ref = b_ref[i, r0 + s - 1:r0 + s, :]
                zeros = jnp.zeros((s, HG_KDIM), F32)
                q_rows += [zeros, q_ref[i, up, :] * jnp.exp2(b_ref[i, up, :] - ref)]
                k_rows += [k_ref[i, lo, :] * jnp.exp2(ref - b_ref[i, lo, :]), zeros]
            ql = jnp.concatenate(q_rows, axis=0)
            kl = jnp.concatenate(k_rows, axis=0)
        else:
            n_fine = 1 + sum(1 for s in levels[:lvl] if 1 < s < HG_VPU_MIN_HALF)
            ex = jnp.exp2(fine[i][n_fine * c:(n_fine + 1) * c])
            ql = q_ref[i] * ex
            kl = k_ref[i] * ex
        ql_ref[i, lvl] = ql.astype(BF16)
        klt_ref[i, lvl] = kl.T

    zero = jnp.zeros((HG_K, c), BF16)

    def head_pair(i, pair):
        psl = slice(2 * pair * HG_K, (2 * pair + 2) * HG_K)
        def block_diag(lvl):
            kt = klt_ref[i, lvl, psl, :].astype(BF16)
            return jnp.concatenate(
                [jnp.concatenate([kt[:HG_K], zero], axis=1),
                 jnp.concatenate([zero, kt[HG_K:]], axis=1)], axis=0)

        att = _dot(ql_ref[i, 0, :, psl], block_diag(0))
        for lvl in range(1, n_levels - 1):
            att = att + _dot(ql_ref[i, lvl, :, psl], block_diag(lvl)) * mask_ref[lvl]
        kbd = block_diag(n_levels)
        for lvl in (n_levels - 1, n_levels):
            att = att + _dot(ql_ref[i, lvl, :, psl], kbd) * mask_ref[lvl]
        for sub in range(2):
            h = 2 * pair + sub
            sl = slice(h * HG_K, (h + 1) * HG_K)
            vh = v_ref[i, :, sl]
            s_t = state_ref[i, h]
            o = _dot_nt(qe_ref[i, :, sl], s_t.astype(BF16))
            o = o + _dot(att[:, sub * c:(sub + 1) * c].astype(BF16), vh)
            state_ref[i, h] = s_t * state_decay[i][:, sl] + _dot_tn(vh, ke_ref[i, :, sl])
            o_ref[i, :, sl] = (_rms(o, nw_ref[...]) * gate_ref[i, :, sl]).astype(o_ref.dtype)

    project(0)
    project(1)
    for i in seqs:
        cumsums(i)
    n_pairs = HG_HEADS // 2
    per_pair = -(-(n_levels + 1) // n_pairs)
    for lvl in range(n_levels + 1):
        level(0, lvl)
        if 2 + lvl < HG_PROJ_PIECES:
            project(2 + lvl)
    for i in seqs:
        for pair in range(n_pairs):
            head_pair(i, pair)
            if i + 1 < n_seq:
                for lvl in range(pair * per_pair, min((pair + 1) * per_pair, n_levels + 1)):
                    level(i + 1, lvl)


def _hgrn2(x, norm_mix_w, w_hg, lb_params, norm_w):
    n_seq, seqlen, d = x.shape
    c = HG_CHUNK
    n_chunks = seqlen // c
    n_levels = len(_hgrn2_levels(c))
    sums, masks = _hgrn2_tables(c)
    consts = (norm_mix_w, w_hg, lb_params, norm_w, sums, masks)
    x_spec, o_spec = _lookahead_specs(n_seq, n_chunks, c, d, HG_VDIM)
    wide = lambda dt, *lead: pltpu.VMEM((n_seq, *lead, c, HG_KDIM), dt)
    return pl.pallas_call(
        _hgrn2_kernel,
        grid=(n_chunks + 1,),
        in_specs=[x_spec] + [_const_spec(a.shape) for a in consts],
        out_specs=o_spec,
        out_shape=jax.ShapeDtypeStruct((n_seq, seqlen, HG_VDIM), BF16),
        scratch_shapes=[pltpu.VMEM((n_seq * c, HG_PROJ), F32),
                        pltpu.VMEM((n_seq, HG_HEADS, HG_V, HG_K), F32),
                        pltpu.VMEM((n_seq, 2 * c, HG_KDIM), BF16),
                        wide(F32), wide(F32), wide(F32),
                        wide(BF16, n_levels + 1),
                        pltpu.VMEM((n_seq, n_levels + 1, HG_KDIM, c), F32),
                        wide(BF16), wide(BF16),
                        wide(BF16), wide(F32)],
        compiler_params=_cparams(("arbitrary",)),
        name="hgrn2",
    )(x, *consts)


def _attn_kernel(x_ref, ya_ref, ob_ref, wout_ref, nw_ref, wq_ref, km_ref, vm_ref, wo_ref,
                 o_ref, ox_ref):
    x1 = x_ref[...] + _dot(ya_ref[...], wout_ref[0:SSD_DIM, :]) \
        + _dot(ob_ref[...], wout_ref[SSD_DIM:, :])
    h = _rms(x1, nw_ref[...]).astype(BF16)
    qx = (_dot(h, wq_ref[...]) * (XA_HEAD_DIM ** -0.5)).astype(BF16)
    for hd in range(XA_HEADS):
        sl = slice(hd * XA_HEAD_DIM, (hd + 1) * XA_HEAD_DIM)
        sc = _dot_nt(qx[:, sl], km_ref[0, :, sl])
        p = jnp.exp(sc - jnp.max(sc, axis=-1, keepdims=True))
        denom = jnp.sum(p, axis=-1, keepdims=True)
        ox_ref[:, sl] = (_dot(p.astype(BF16), vm_ref[0, :, sl]) / denom).astype(BF16)
    o_ref[...] = x1 + _dot(ox_ref[...], wo_ref[...])


def _attn(x2d, ya, ob, w_out, norm_w, wq, km, vm, wo, seqlen):
    t, d = x2d.shape
    tm = min(ATTN_ROWS, seqlen)
    tpb = seqlen // tm
    mlen = km.shape[1]
    row = lambda n: pl.BlockSpec((tm, n), lambda i: (i, 0))
    mem = pl.BlockSpec((1, mlen, d), lambda i: (i // tpb, 0, 0))
    return pl.pallas_call(
        _attn_kernel,
        grid=(t // tm,),
        in_specs=[row(d), row(SSD_DIM), row(HG_VDIM), _const_spec(w_out.shape),
                  _const_spec(norm_w.shape), _const_spec(wq.shape), mem, mem,
                  _const_spec(wo.shape)],
        out_specs=row(d),
        out_shape=jax.ShapeDtypeStruct((t, d), F32),
        scratch_shapes=[pltpu.VMEM((tm, d), BF16)],
        compiler_params=_cparams(("arbitrary",)),
        name="attn",
    )(x2d, ya, ob, w_out, norm_w, wq, km, vm, wo)


def _ffn_kernel(x_ref, nw_ref, wg_ref, wu_ref, wd_ref, nf_ref, o_ref, acc_ref):
    x = x_ref[...]
    h = _rms(x, nw_ref[...]).astype(BF16)
    acc_ref[...] = x
    n_cols = wg_ref.shape[1]
    for c0 in range(0, n_cols, FFN_COLS):
        g = _dot(h, wg_ref[:, c0:c0 + FFN_COLS])
        u = _dot(h, wu_ref[:, c0:c0 + FFN_COLS])
        a = (_silu(g) * u).astype(BF16)
        acc_ref[...] += _dot(a, wd_ref[c0:c0 + FFN_COLS, :])
    o_ref[...] = _rms(acc_ref[...], nf_ref[...])


def _ffn(x2d, norm_w, wg, wu, wd, norm_final):
    t, d = x2d.shape
    tm = min(FFN_ROWS, t)
    row = pl.BlockSpec((tm, d), lambda i: (i, 0))
    return pl.pallas_call(
        _ffn_kernel,
        grid=(t // tm,),
        in_specs=[row, _const_spec(norm_w.shape), _const_spec(wg.shape), _const_spec(wu.shape),
                  _const_spec(wd.shape), _const_spec(norm_final.shape)],
        out_specs=row,
        out_shape=jax.ShapeDtypeStruct((t, d), F32),
        scratch_shapes=[pltpu.VMEM((tm, d), F32)],
        compiler_params=_cparams(("arbitrary",)),
        name="ffn",
    )(x2d, norm_w, wg, wu, wd, norm_final)


def _pad_lanes(v):
    return jnp.pad(v.astype(F32), (0, LANES - v.shape[0])).reshape(1, LANES)


def kernel(x, mem, norm_mix_w, w_in, conv_w, conv_b, dt_bias, a_log, d_skip, ssd_norm_w,
           hg_lower_bounds, hg_norm_w, w_out, norm_xa_w, norm_mem_w, xa_wq, xa_wkv, xa_wo,
           norm_ffn_w, ffn_w_gate, ffn_w_up, ffn_w_down, norm_final_w):
    bsz, seqlen, d = x.shape
    assert d == D_MODEL and norm_mix_w.shape[0] == 1, "single-layer block of width 1024"
    assert hg_lower_bounds.shape[0] == 2
    assert seqlen % SSD_CHUNK == 0 and seqlen % HG_CHUNK == 0
    t = bsz * seqlen
    x2d = x.reshape(t, d)
    row = lambda v: v.reshape(1, -1).astype(F32)

    w_ssd, w_hg = _w_split(w_in)

    expand = (np.arange(LANES)[:, None] == (np.arange(SSD_DIM)[None, :] // SSD_HEAD_DIM))
    expand = jnp.asarray(expand, BF16)
    d_skip_w = jnp.repeat(d_skip[0].astype(F32), SSD_HEAD_DIM).reshape(1, SSD_DIM)
    y_a = _ssd(x, row(norm_mix_w[0]), w_ssd, conv_w[0].astype(F32), row(conv_b[0]),
               _pad_lanes(dt_bias[0]), _pad_lanes(a_log[0]), d_skip_w, row(ssd_norm_w[0]),
               expand).reshape(t, SSD_DIM)
    o_b = _hgrn2(x, row(norm_mix_w[0]), w_hg, hg_lower_bounds.astype(F32),
                 row(hg_norm_w[0])).reshape(t, HG_VDIM)

    km, vm = _mem_kv(mem, row(norm_mem_w[0]), xa_wkv[0].astype(BF16))
    x2 = _attn(x2d, y_a, o_b, w_out[0].astype(BF16), row(norm_xa_w[0]), xa_wq[0].astype(BF16),
               km, vm, xa_wo[0].astype(BF16), seqlen)
    out = _ffn(x2, row(norm_ffn_w[0]), ffn_w_gate[0].astype(BF16), ffn_w_up[0].astype(BF16),
               ffn_w_down[0].astype(BF16), row(norm_final_w))
    return out.reshape(bsz, seqlen, d)
```

```python
import math

import numpy as np
import jax
import jax.numpy as jnp
from jax import lax
from jax.experimental import pallas as pl
from jax.experimental.pallas import tpu as pltpu

F32 = jnp.float32
BF16 = jnp.bfloat16
EPS = 1e-6
LOG2E = math.log2(math.e)

D_MODEL = 1024
SSD_HEADS = 16
SSD_HEAD_DIM = 64
SSD_DIM = SSD_HEADS * SSD_HEAD_DIM
SSD_GROUPS = 2
SSD_STATE = 128
SSD_CONV = 4
SSD_CHUNK = 128
SSD_CONV_CH = SSD_DIM + 2 * SSD_GROUPS * SSD_STATE
HG_HEADS = 8
HG_K = 128
HG_V = 128
HG_KDIM = HG_HEADS * HG_K
HG_VDIM = HG_HEADS * HG_V
XA_HEADS = 4
XA_HEAD_DIM = D_MODEL // XA_HEADS

LANES = 128
BF16_SUBLANES = 16
CONV_TAIL = BF16_SUBLANES
VMEM_LIMIT_BYTES = 56 * 1024 * 1024

HG_CHUNK = 128
ATTN_ROWS = 512
FFN_ROWS = 512
FFN_COLS = 256

SSD_PROJ = SSD_DIM + SSD_CONV_CH + LANES
SSD_PROJ_PIECE = 512
HG_PROJ = 2 * HG_KDIM + 2 * HG_VDIM


def _cparams(sem):
    return pltpu.CompilerParams(dimension_semantics=sem, vmem_limit_bytes=VMEM_LIMIT_BYTES)


def _const_spec(shape):
    zeros = (0,) * len(shape)
    return pl.BlockSpec(shape, lambda *_: zeros)


def _dot(a, b):
    return jnp.dot(a, b, preferred_element_type=F32)


def _dot_nt(a, b):
    return lax.dot_general(a, b, (((1,), (1,)), ((), ())), preferred_element_type=F32)


def _dot_tn(a, b):
    return lax.dot_general(a, b, (((0,), (0,)), ((), ())), preferred_element_type=F32)


def _sigmoid(x):
    return 1.0 / (1.0 + jnp.exp(-x))


def _silu(x):
    return x * _sigmoid(x)


def _rms(x, w):
    ms = jnp.mean(x * x, axis=-1, keepdims=True)
    return x * lax.rsqrt(ms + EPS) * w


def _split_hi_lo(x):
    hi = x.astype(BF16)
    lo = (x - hi.astype(F32)).astype(BF16)
    return hi, lo


def _mem_kv_kernel(mem_ref, nw_ref, wkv_ref, wq_ref, wo_ref, wqk_ref, wvo_ref):
    m = _rms(mem_ref[0], nw_ref[...]).astype(BF16)
    kv = _dot(m, wkv_ref[...])
    km = kv[:, :D_MODEL].astype(BF16)
    vm = kv[:, D_MODEL:].astype(BF16)
    mlen = km.shape[0]
    for hd in range(XA_HEADS):
        sl = slice(hd * XA_HEAD_DIM, (hd + 1) * XA_HEAD_DIM)
        ml = slice(hd * mlen, (hd + 1) * mlen)
        wqk_ref[0, :, ml] = (_dot_nt(wq_ref[:, sl], km[:, sl]) * (XA_HEAD_DIM ** -0.5)).astype(BF16)
        wvo_ref[0, ml, :] = _dot(vm[:, sl], wo_ref[sl, :]).astype(BF16)


def _mem_kv(mem, norm_w, wkv, wq, wo):
    bsz, mlen, d = mem.shape
    return pl.pallas_call(
        _mem_kv_kernel,
        grid=(bsz,),
        in_specs=[pl.BlockSpec((1, mlen, d), lambda b: (b, 0, 0)),
                  _const_spec((1, d)), _const_spec((d, 2 * d)),
                  _const_spec(wq.shape), _const_spec(wo.shape)],
        out_specs=[pl.BlockSpec((1, d, XA_HEADS * mlen), lambda b: (b, 0, 0)),
                   pl.BlockSpec((1, XA_HEADS * mlen, d), lambda b: (b, 0, 0))],
        out_shape=[jax.ShapeDtypeStruct((bsz, d, XA_HEADS * mlen), BF16),
                   jax.ShapeDtypeStruct((bsz, XA_HEADS * mlen, d), BF16)],
        compiler_params=_cparams(("arbitrary",)),
        name="mem_kv",
    )(mem, norm_w, wkv, wq, wo)


def _ssd_kernel(x_ref, nmw_ref, w_ref, convw_ref, convb_ref, dtb_ref, alog_ref,
                dskip_ref, nw_ref, expand_ref, shift_ref, o_ref,
                proj_ref, ext_ref, state_ref, y_ref, zs_ref):
    n_seq, q = x_ref.shape[0], x_ref.shape[1]
    gw = SSD_DIM // SSD_GROUPS
    heads_per_group = SSD_HEADS // SSD_GROUPS
    step = pl.program_id(0)
    seqs = range(n_seq)

    @pl.when(step == 0)
    def _():
        proj_ref[...] = jnp.zeros_like(proj_ref)

    @pl.when(step <= 1)
    def _():
        ext_ref[:, 0:CONV_TAIL, :] = jnp.zeros((n_seq, CONV_TAIL, SSD_CONV_CH), BF16)
        state_ref[...] = jnp.zeros_like(state_ref)

    @pl.when(step > 1)
    def _():
        ext_ref[:, 0:CONV_TAIL, :] = ext_ref[:, q:q + CONV_TAIL, :]

    row = lax.broadcasted_iota(jnp.int32, (q, q), 0)
    colm = lax.broadcasted_iota(jnp.int32, (q, q), 1)
    causal = row >= colm
    tri = causal.astype(BF16)
    lane = lax.broadcasted_iota(jnp.int32, (q, LANES), 1)
    left = lane < SSD_HEAD_DIM

    h_next = _rms(x_ref[...].reshape(n_seq * q, D_MODEL), nmw_ref[...]).astype(BF16)
    n_pieces = -(-SSD_PROJ // SSD_PROJ_PIECE)

    def project(j):
        cols = slice(j * SSD_PROJ_PIECE, min((j + 1) * SSD_PROJ_PIECE, SSD_PROJ))
        proj_ref[:, cols] = _dot(h_next, w_ref[:, cols])

    xs, xdt, xw, b_all, c_all, acum, acum_t, dfs_w, cd_w = ([None] * n_seq for _ in range(9))
    u_cur, dtr = [None] * n_seq, [None] * n_seq
    for i in seqs:
        rows = slice(i * q, (i + 1) * q)
        zs_ref[i] = _silu(proj_ref[rows, 0:SSD_DIM])
        u_cur[i] = proj_ref[rows, SSD_DIM:SSD_DIM + SSD_CONV_CH].astype(BF16)
        ext_ref[i, CONV_TAIL:CONV_TAIL + q, :] = u_cur[i]
        dtr[i] = proj_ref[rows, SSD_DIM + SSD_CONV_CH:] + dtb_ref[...]

    for i in seqs:
        shifted = _dot(shift_ref[...], ext_ref[i])
        project(2 * i)
        acc = convb_ref[...] + convw_ref[SSD_CONV - 1:SSD_CONV, :] * u_cur[i].astype(F32)
        for k in range(SSD_CONV - 1):
            acc = acc + convw_ref[k:k + 1, :] * shifted[k * q:(k + 1) * q]
        xbc = _silu(acc)
        xs[i] = xbc[:, :SSD_DIM]
        b_all[i] = xbc[:, SSD_DIM:SSD_DIM + SSD_GROUPS * SSD_STATE]
        c_all[i] = xbc[:, SSD_DIM + SSD_GROUPS * SSD_STATE:]

        dt = jnp.maximum(dtr[i], 0.0) + jnp.log(1.0 + jnp.exp(-jnp.abs(dtr[i])))
        da = dt * (-jnp.exp(alog_ref[...]) * LOG2E)
        da_hi, da_lo = _split_hi_lo(da)
        da_lo2 = (da - da_hi.astype(F32) - da_lo.astype(F32)).astype(BF16)
        acum[i] = _dot(tri, da_hi) + _dot(tri, da_lo) + _dot(tri, da_lo2)
        project(2 * i + 1)
        acum_t[i] = acum[i].T
        a_last = acum[i][q - 1:q, :]
        dec_from_start = jnp.exp2(acum[i])
        dec_to_end = jnp.exp2(a_last - acum[i])
        chunk_decay = jnp.broadcast_to(jnp.exp2(a_last), (BF16_SUBLANES, LANES))

        stack = jnp.concatenate([dt, dec_from_start, dec_to_end], axis=0).astype(BF16)
        wide = _dot(stack, expand_ref[...])
        cd_hi, cd_lo = _split_hi_lo(chunk_decay)
        cd_w[i] = (_dot(cd_hi, expand_ref[...]) + _dot(cd_lo, expand_ref[...]))[0:1]
        dfs_w[i] = wide[q:2 * q]
        xdt[i] = xs[i] * wide[0:q]
        xw[i] = (xdt[i] * wide[2 * q:3 * q]).astype(BF16)

    for g in range(SSD_GROUPS):
        gsl = slice(g * gw, (g + 1) * gw)
        for i in seqs:
            b_g = b_all[i][:, g * SSD_STATE:(g + 1) * SSD_STATE].astype(BF16)
            c_g = c_all[i][:, g * SSD_STATE:(g + 1) * SSD_STATE].astype(BF16)
            cb = _dot_nt(c_g, b_g)
            s_in = state_ref[i, g]
            y_ref[i, :, gsl] = _dot(c_g, s_in.astype(BF16)) * dfs_w[i][:, gsl]
            state_ref[i, g] = s_in * cd_w[i][:, gsl] + _dot_tn(b_g, xw[i][:, gsl])
            for pair in range(heads_per_group // 2):
                h0 = g * heads_per_group + 2 * pair
                ms = []
                for h in (h0, h0 + 1):
                    seg = acum[i][:, h:h + 1] - acum_t[i][h:h + 1, :]
                    l_dec = jnp.exp2(jnp.where(causal, seg, -jnp.inf))
                    ms.append((cb * l_dec).astype(BF16))
                lhs = jnp.concatenate(ms, axis=1)
                psl = slice(h0 * SSD_HEAD_DIM, (h0 + 2) * SSD_HEAD_DIM)
                xp = xdt[i][:, psl]
                rhs = jnp.concatenate([jnp.where(left, xp, 0.0), jnp.where(left, 0.0, xp)],
                                      axis=0).astype(BF16)
                y_ref[i, :, psl] += _dot(lhs, rhs)

    for j in range(2 * n_seq, n_pieces):
        project(j)
    for i in seqs:
        yz = (y_ref[i] + dskip_ref[...] * xs[i]) * zs_ref[i]
        for g in range(SSD_GROUPS):
            gsl = slice(g * gw, (g + 1) * gw)
            o_ref[i, :, gsl] = _rms(yz[:, gsl], nw_ref[:, gsl]).astype(o_ref.dtype)


def _conv_shift_matrix(q):
    s = np.zeros(((SSD_CONV - 1) * q, CONV_TAIL + q), np.float32)
    for k in range(SSD_CONV - 1):
        for t in range(q):
            s[k * q + t, CONV_TAIL + t - (SSD_CONV - 1) + k] = 1.0
    return jnp.asarray(s, BF16)


def _lookahead_specs(n_seq, n_chunks, rows, width_in, width_out):
    x_spec = pl.BlockSpec((n_seq, rows, width_in),
                          lambda s: (0, jnp.minimum(s, n_chunks - 1), 0))
    o_spec = pl.BlockSpec((n_seq, rows, width_out), lambda s: (0, jnp.maximum(s - 1, 0), 0))
    return x_spec, o_spec


def _ssd(x, norm_mix_w, w_ssd, conv_w, conv_b, dt_bias, a_log, d_skip_w, norm_w, expand):
    n_seq, seqlen, d = x.shape
    q = SSD_CHUNK
    n_chunks = seqlen // q
    shift = _conv_shift_matrix(q)
    x_spec, o_spec = _lookahead_specs(n_seq, n_chunks, q, d, SSD_DIM)
    consts = (norm_mix_w, w_ssd, conv_w, conv_b, dt_bias, a_log, d_skip_w, norm_w, expand, shift)
    return pl.pallas_call(
        _ssd_kernel,
        grid=(n_chunks + 1,),
        in_specs=[x_spec] + [_const_spec(c.shape) for c in consts],
        out_specs=o_spec,
        out_shape=jax.ShapeDtypeStruct((n_seq, seqlen, SSD_DIM), BF16),
        scratch_shapes=[pltpu.VMEM((n_seq * q, SSD_PROJ), F32),
                        pltpu.VMEM((n_seq, CONV_TAIL + q, SSD_CONV_CH), BF16),
                        pltpu.VMEM((n_seq, SSD_GROUPS, SSD_STATE, SSD_DIM // SSD_GROUPS), F32),
                        pltpu.VMEM((n_seq, q, SSD_DIM), F32),
                        pltpu.VMEM((n_seq, q, SSD_DIM), F32)],
        compiler_params=_cparams(("arbitrary",)),
        name="ssd",
    )(x, *consts)


HG_PROJ_PIECES = 8
HG_VPU_MIN_HALF = 8


def _hgrn2_levels(c):
    levels = []
    s = c // 2
    while s >= 1:
        levels.append(s)
        s //= 2
    return levels


def _hgrn2_tables(c):
    idx = np.arange(c)
    blocks = [np.tril(np.ones((c, c), np.float32))]
    masks = []
    for s in _hgrn2_levels(c):
        mid = (idx // (2 * s)) * (2 * s) + s
        upper = idx >= mid
        if 1 < s < HG_VPU_MIN_HALF:
            a = np.zeros((c, c), np.float32)
            for i in range(c):
                if upper[i]:
                    a[i, mid[i]:i + 1] = 1.0
                else:
                    a[i, i + 1:mid[i]] = 1.0
            blocks.append(a)
        same = (idx[:, None] // (2 * s)) == (idx[None, :] // (2 * s))
        masks.append((same & upper[:, None] & (~upper)[None, :]).astype(np.float32))
    masks.append(np.eye(c, dtype=np.float32))
    sums = np.concatenate(blocks, axis=0)
    sums2 = np.concatenate([sums, sums], axis=1)
    masks = np.stack(masks)
    masks2 = np.concatenate([masks, masks], axis=2)
    return jnp.asarray(sums2, BF16), jnp.asarray(masks2, F32)


def _hgrn2_kernel(x_ref, nmw_ref, w_ref, lbp_ref, nw_ref, sums_ref, mask_ref, o_ref,
                  proj_ref, state_ref, g2_ref, q_ref, k_ref, b_ref, ql_ref, klt_ref, qe_ref,
                  ke_ref, v_ref, gate_ref):
    n_seq, c = x_ref.shape[0], x_ref.shape[1]
    levels = _hgrn2_levels(c)
    n_levels = len(levels)
    step = pl.program_id(0)
    seqs = range(n_seq)

    @pl.when(step == 0)
    def _():
        proj_ref[...] = jnp.zeros_like(proj_ref)

    @pl.when(step <= 1)
    def _():
        state_ref[...] = jnp.zeros_like(state_ref)

    lbp = lbp_ref[...]
    e = jnp.exp(lbp - jnp.max(lbp, axis=0, keepdims=True))
    lb = e[0:1, :] / jnp.sum(e, axis=0, keepdims=True)

    for i in seqs:
        rows = slice(i * c, (i + 1) * c)
        q_ref[i] = _silu(proj_ref[rows, 0:HG_KDIM])
        fg = lb + (1.0 - lb) * _sigmoid(proj_ref[rows, HG_KDIM:2 * HG_KDIM])
        v_ref[i] = proj_ref[rows, 2 * HG_KDIM:2 * HG_KDIM + HG_VDIM].astype(BF16)
        gate_ref[i] = _silu(proj_ref[rows, 2 * HG_KDIM + HG_VDIM:])
        g_hi, g_lo = _split_hi_lo(jnp.log2(fg))
        g2_ref[i, 0:c, :] = g_hi
        g2_ref[i, c:2 * c, :] = g_lo
        k_ref[i] = 1.0 - fg
        ql_ref[i, n_levels - 1] = (q_ref[i] * fg).astype(BF16)

    h_next = _rms(x_ref[...].reshape(n_seq * c, D_MODEL), nmw_ref[...]).astype(BF16)
    piece = HG_PROJ // HG_PROJ_PIECES

    def project(j):
        cols = slice(j * piece, (j + 1) * piece)
        proj_ref[:, cols] = _dot(h_next, w_ref[:, cols])

    fine, state_decay = [None] * n_seq, [None] * n_seq

    def cumsums(i):
        fine[i] = _dot(sums_ref[...], g2_ref[i])
        b_ref[i] = fine[i][0:c]
        b_last = b_ref[i, c - 1:c, :]
        qe_ref[i] = (q_ref[i] * jnp.exp2(b_ref[i])).astype(BF16)
        ke_ref[i] = (k_ref[i] * jnp.exp2(b_last - b_ref[i])).astype(BF16)
        state_decay[i] = jnp.exp2(b_last)

    def level(i, lvl):
        if lvl == n_levels - 1:
            return
        if lvl == n_levels:
            ql, kl = q_ref[i], k_ref[i]
        elif levels[lvl] >= HG_VPU_MIN_HALF:
            s = levels[lvl]
            q_rows, k_rows = [], []
            for r0 in range(0, c, 2 * s):
                lo = slice(r0, r0 + s)
                up = slice(r0 + s, r0 + 2 * s)
                ref = b_ref[i, r0 + s - 1:r0 + s, :]
                zeros = jnp.zeros((s, HG_KDIM), F32)
                q_rows += [zeros, q_ref[i, up, :] * jnp.exp2(b_ref[i, up, :] - ref)]
                k_rows += [k_ref[i, lo, :] * jnp.exp2(ref - b_ref[i, lo, :]), zeros]
            ql = jnp.concatenate(q_rows, axis=0)
            kl = jnp.concatenate(k_rows, axis=0)
        else:
            n_fine = 1 + sum(1 for s in levels[:lvl] if 1 < s < HG_VPU_MIN_HALF)
            ex = jnp.exp2(fine[i][n_fine * c:(n_fine + 1) * c])
            ql = q_ref[i] * ex
            kl = k_ref[i] * ex
        ql_ref[i, lvl] = ql.astype(BF16)
        klt_ref[i, lvl] = kl.T

    zero = jnp.zeros((HG_K, c), BF16)

    def head_pair(i, pair):
        psl = slice(2 * pair * HG_K, (2 * pair + 2) * HG_K)
        def block_diag(lvl):
            kt = klt_ref[i, lvl, psl, :].astype(BF16)
            return jnp.concatenate(
                [jnp.concatenate([kt[:HG_K], zero], axis=1),
                 jnp.concatenate([zero, kt[HG_K:]], axis=1)], axis=0)

        att = _dot(ql_ref[i, 0, :, psl], block_diag(0))
        for lvl in range(1, n_levels - 1):
            att = att + _dot(ql_ref[i, lvl, :, psl], block_diag(lvl)) * mask_ref[lvl]
        kbd = block_diag(n_levels)
        for lvl in (n_levels - 1, n_levels):
            att = att + _dot(ql_ref[i, lvl, :, psl], kbd) * mask_ref[lvl]
        for sub in range(2):
            h = 2 * pair + sub
            sl = slice(h * HG_K, (h + 1) * HG_K)
            vh = v_ref[i, :, sl]
            s_t = state_ref[i, h]
            o = _dot_nt(qe_ref[i, :, sl], s_t.astype(BF16))
            o = o + _dot(att[:, sub * c:(sub + 1) * c].astype(BF16), vh)
            state_ref[i, h] = s_t * state_decay[i][:, sl] + _dot_tn(vh, ke_ref[i, :, sl])
            o_ref[i, :, sl] = (_rms(o, nw_ref[...]) * gate_ref[i, :, sl]).astype(o_ref.dtype)

    project(0)
    project(1)
    for i in seqs:
        cumsums(i)
    n_pairs = HG_HEADS // 2
    per_pair = -(-(n_levels + 1) // n_pairs)
    for lvl in range(n_levels + 1):
        level(0, lvl)
        if 2 + lvl < HG_PROJ_PIECES:
            project(2 + lvl)
    for i in seqs:
        for pair in range(n_pairs):
            head_pair(i, pair)
            if i + 1 < n_seq:
                for lvl in range(pair * per_pair, min((pair + 1) * per_pair, n_levels + 1)):
                    level(i + 1, lvl)


def _hgrn2(x, norm_mix_w, w_hg, lb_params, norm_w):
    n_seq, seqlen, d = x.shape
    c = HG_CHUNK
    n_chunks = seqlen // c
    n_levels = len(_hgrn2_levels(c))
    sums, masks = _hgrn2_tables(c)
    consts = (norm_mix_w, w_hg, lb_params, norm_w, sums, masks)
    x_spec, o_spec = _lookahead_specs(n_seq, n_chunks, c, d, HG_VDIM)
    wide = lambda dt, *lead: pltpu.VMEM((n_seq, *lead, c, HG_KDIM), dt)
    return pl.pallas_call(
        _hgrn2_kernel,
        grid=(n_chunks + 1,),
        in_specs=[x_spec] + [_const_spec(a.shape) for a in consts],
        out_specs=o_spec,
        out_shape=jax.ShapeDtypeStruct((n_seq, seqlen, HG_VDIM), BF16),
        scratch_shapes=[pltpu.VMEM((n_seq * c, HG_PROJ), F32),
                        pltpu.VMEM((n_seq, HG_HEADS, HG_V, HG_K), F32),
                        pltpu.VMEM((n_seq, 2 * c, HG_KDIM), BF16),
                        wide(F32), wide(F32), wide(F32),
                        wide(BF16, n_levels + 1),
                        pltpu.VMEM((n_seq, n_levels + 1, HG_KDIM, c), F32),
                        wide(BF16), wide(BF16),
                        wide(BF16), wide(F32)],
        compiler_params=_cparams(("arbitrary",)),
        name="hgrn2",
    )(x, *consts)


def _attn_kernel(x_ref, ya_ref, ob_ref, wout_ref, nw_ref, wqk_ref, wvo_ref, o_ref, p_ref):
    x1 = x_ref[...] + _dot(ya_ref[...], wout_ref[0:SSD_DIM, :]) \
        + _dot(ob_ref[...], wout_ref[SSD_DIM:, :])
    h = _rms(x1, nw_ref[...]).astype(BF16)
    scores = _dot(h, wqk_ref[0])
    mlen = wqk_ref.shape[2] // XA_HEADS
    for hd in range(XA_HEADS):
        ml = slice(hd * mlen, (hd + 1) * mlen)
        sc = scores[:, ml]
        p = jnp.exp(sc - jnp.max(sc, axis=-1, keepdims=True))
        p_ref[:, ml] = (p / jnp.sum(p, axis=-1, keepdims=True)).astype(BF16)
    o_ref[...] = x1 + _dot(p_ref[...], wvo_ref[0])


def _attn(x2d, ya, ob, w_out, norm_w, wqk, wvo, seqlen):
    t, d = x2d.shape
    tm = min(ATTN_ROWS, seqlen)
    tpb = seqlen // tm
    n_mem = wqk.shape[2]
    row = lambda n: pl.BlockSpec((tm, n), lambda i: (i, 0))
    return pl.pallas_call(
        _attn_kernel,
        grid=(t // tm,),
        in_specs=[row(d), row(SSD_DIM), row(HG_VDIM), _const_spec(w_out.shape),
                  _const_spec(norm_w.shape),
                  pl.BlockSpec((1, d, n_mem), lambda i: (i // tpb, 0, 0)),
                  pl.BlockSpec((1, n_mem, d), lambda i: (i // tpb, 0, 0))],
        out_specs=row(d),
        out_shape=jax.ShapeDtypeStruct((t, d), F32),
        scratch_shapes=[pltpu.VMEM((tm, n_mem), BF16)],
        compiler_params=_cparams(("arbitrary",)),
        name="attn",
    )(x2d, ya, ob, w_out, norm_w, wqk, wvo)


def _ffn_kernel(x_ref, nw_ref, wg_ref, wu_ref, wd_ref, nf_ref, o_ref, acc_ref):
    x = x_ref[...]
    h = _rms(x, nw_ref[...]).astype(BF16)
    acc_ref[...] = x
    n_cols = wg_ref.shape[1]
    for c0 in range(0, n_cols, FFN_COLS):
        g = _dot(h, wg_ref[:, c0:c0 + FFN_COLS])
        u = _dot(h, wu_ref[:, c0:c0 + FFN_COLS])
        a = (_silu(g) * u).astype(BF16)
        acc_ref[...] += _dot(a, wd_ref[c0:c0 + FFN_COLS, :])
    o_ref[...] = _rms(acc_ref[...], nf_ref[...])


def _ffn(x2d, norm_w, wg, wu, wd, norm_final):
    t, d = x2d.shape
    tm = min(FFN_ROWS, t)
    row = pl.BlockSpec((tm, d), lambda i: (i, 0))
    return pl.pallas_call(
        _ffn_kernel,
        grid=(t // tm,),
        in_specs=[row, _const_spec(norm_w.shape), _const_spec(wg.shape), _const_spec(wu.shape),
                  _const_spec(wd.shape), _const_spec(norm_final.shape)],
        out_specs=row,
        out_shape=jax.ShapeDtypeStruct((t, d), F32),
        scratch_shapes=[pltpu.VMEM((tm, d), F32)],
        compiler_params=_cparams(("arbitrary",)),
        name="ffn",
    )(x2d, norm_w, wg, wu, wd, norm_final)


def _pad_lanes(v):
    return jnp.pad(v.astype(F32), (0, LANES - v.shape[0])).reshape(1, LANES)


def kernel(x, mem, norm_mix_w, w_in, conv_w, conv_b, dt_bias, a_log, d_skip, ssd_norm_w,
           hg_lower_bounds, hg_norm_w, w_out, norm_xa_w, norm_mem_w, xa_wq, xa_wkv, xa_wo,
           norm_ffn_w, ffn_w_gate, ffn_w_up, ffn_w_down, norm_final_w):
    bsz, seqlen, d = x.shape
    assert d == D_MODEL and norm_mix_w.shape[0] == 1, "single-layer block of width 1024"
    assert hg_lower_bounds.shape[0] == 2
    assert seqlen % SSD_CHUNK == 0 and seqlen % HG_CHUNK == 0
    t = bsz * seqlen
    x2d = x.reshape(t, d)
    row = lambda v: v.reshape(1, -1).astype(F32)

    s3 = SSD_DIM + SSD_CONV_CH + SSD_HEADS
    w = w_in[0]
    w_ssd = jnp.pad(w[:, :s3], ((0, 0), (0, LANES - SSD_HEADS))).astype(BF16)
    w_hg = w[:, s3:].astype(BF16)

    expand = (np.arange(LANES)[:, None] == (np.arange(SSD_DIM)[None, :] // SSD_HEAD_DIM))
    expand = jnp.asarray(expand, BF16)
    d_skip_w = jnp.repeat(d_skip[0].astype(F32), SSD_HEAD_DIM).reshape(1, SSD_DIM)
    y_a = _ssd(x, row(norm_mix_w[0]), w_ssd, conv_w[0].astype(F32), row(conv_b[0]),
               _pad_lanes(dt_bias[0]), _pad_lanes(a_log[0]), d_skip_w, row(ssd_norm_w[0]),
               expand).reshape(t, SSD_DIM)
    o_b = _hgrn2(x, row(norm_mix_w[0]), w_hg, hg_lower_bounds.astype(F32),
                 row(hg_norm_w[0])).reshape(t, HG_VDIM)

    wqk, wvo = _mem_kv(mem, row(norm_mem_w[0]), xa_wkv[0].astype(BF16), xa_wq[0].astype(BF16),
                       xa_wo[0].astype(BF16))
    x2 = _attn(x2d, y_a, o_b, w_out[0].astype(BF16), row(norm_xa_w[0]), wqk, wvo, seqlen)
    out = _ffn(x2, row(norm_ffn_w[0]), ffn_w_gate[0].astype(BF16), ffn_w_up[0].astype(BF16),
               ffn_w_down[0].astype(BF16), row(norm_final_w))
    return out.reshape(bsz, seqlen, d)
```

```python
import math

import numpy as np
import jax
import jax.numpy as jnp
from jax import lax
from jax.experimental import pallas as pl
from jax.experimental.pallas import tpu as pltpu

F32 = jnp.float32
BF16 = jnp.bfloat16
EPS = 1e-6
LOG2E = math.log2(math.e)

D_MODEL = 1024
SSD_HEADS = 16
SSD_HEAD_DIM = 64
SSD_DIM = SSD_HEADS * SSD_HEAD_DIM
SSD_GROUPS = 2
SSD_STATE = 128
SSD_CONV = 4
SSD_CHUNK = 128
SSD_CONV_CH = SSD_DIM + 2 * SSD_GROUPS * SSD_STATE
HG_HEADS = 8
HG_K = 128
HG_V = 128
HG_KDIM = HG_HEADS * HG_K
HG_VDIM = HG_HEADS * HG_V
XA_HEADS = 4
XA_HEAD_DIM = D_MODEL // XA_HEADS

LANES = 128
BF16_SUBLANES = 16
CONV_TAIL = BF16_SUBLANES
VMEM_LIMIT_BYTES = 56 * 1024 * 1024

HG_CHUNK = 128
ATTN_ROWS = 512
FFN_ROWS = 512
FFN_COLS = 256

SSD_PROJ = SSD_DIM + SSD_CONV_CH + LANES
SSD_PROJ_PIECE = 768
HG_PROJ = 2 * HG_KDIM + 2 * HG_VDIM


def _cparams(sem):
    return pltpu.CompilerParams(dimension_semantics=sem, vmem_limit_bytes=VMEM_LIMIT_BYTES)


def _const_spec(shape):
    zeros = (0,) * len(shape)
    return pl.BlockSpec(shape, lambda *_: zeros)


def _dot(a, b):
    return jnp.dot(a, b, preferred_element_type=F32)


def _dot_nt(a, b):
    return lax.dot_general(a, b, (((1,), (1,)), ((), ())), preferred_element_type=F32)


def _dot_tn(a, b):
    return lax.dot_general(a, b, (((0,), (0,)), ((), ())), preferred_element_type=F32)


def _sigmoid(x):
    return 1.0 / (1.0 + jnp.exp(-x))


def _silu(x):
    return x * _sigmoid(x)


def _rms(x, w):
    ms = jnp.mean(x * x, axis=-1, keepdims=True)
    return x * lax.rsqrt(ms + EPS) * w


def _split_hi_lo(x):
    hi = x.astype(BF16)
    lo = (x - hi.astype(F32)).astype(BF16)
    return hi, lo


def _mem_kv_kernel(mem_ref, nw_ref, wkv_ref, wq_ref, wo_ref, wqk_ref, wvo_ref):
    m = _rms(mem_ref[0], nw_ref[...]).astype(BF16)
    kv = _dot(m, wkv_ref[...])
    km = kv[:, :D_MODEL].astype(BF16)
    vm = kv[:, D_MODEL:].astype(BF16)
    mlen = km.shape[0]
    for hd in range(XA_HEADS):
        sl = slice(hd * XA_HEAD_DIM, (hd + 1) * XA_HEAD_DIM)
        ml = slice(hd * mlen, (hd + 1) * mlen)
        wqk_ref[0, :, ml] = (_dot_nt(wq_ref[:, sl], km[:, sl]) * (XA_HEAD_DIM ** -0.5)).astype(BF16)
        wvo_ref[0, ml, :] = _dot(vm[:, sl], wo_ref[sl, :]).astype(BF16)


def _mem_kv(mem, norm_w, wkv, wq, wo):
    bsz, mlen, d = mem.shape
    return pl.pallas_call(
        _mem_kv_kernel,
        grid=(bsz,),
        in_specs=[pl.BlockSpec((1, mlen, d), lambda b: (b, 0, 0)),
                  _const_spec((1, d)), _const_spec((d, 2 * d)),
                  _const_spec(wq.shape), _const_spec(wo.shape)],
        out_specs=[pl.BlockSpec((1, d, XA_HEADS * mlen), lambda b: (b, 0, 0)),
                   pl.BlockSpec((1, XA_HEADS * mlen, d), lambda b: (b, 0, 0))],
        out_shape=[jax.ShapeDtypeStruct((bsz, d, XA_HEADS * mlen), BF16),
                   jax.ShapeDtypeStruct((bsz, XA_HEADS * mlen, d), BF16)],
        compiler_params=_cparams(("arbitrary",)),
        name="mem_kv",
    )(mem, norm_w, wkv, wq, wo)


def _ssd_kernel(x_ref, nmw_ref, w_ref, convw_ref, convb_ref, dtb_ref, alog_ref,
                dskip_ref, nw_ref, expand_ref, shift_ref, o_ref,
                proj_ref, ext_ref, state_ref, y_ref, zs_ref):
    n_seq, q = x_ref.shape[0], x_ref.shape[1]
    gw = SSD_DIM // SSD_GROUPS
    heads_per_group = SSD_HEADS // SSD_GROUPS
    step = pl.program_id(0)
    seqs = range(n_seq)

    @pl.when(step == 0)
    def _():
        proj_ref[...] = jnp.zeros_like(proj_ref)

    @pl.when(step <= 1)
    def _():
        ext_ref[:, 0:CONV_TAIL, :] = jnp.zeros((n_seq, CONV_TAIL, SSD_CONV_CH), BF16)
        state_ref[...] = jnp.zeros_like(state_ref)

    @pl.when(step > 1)
    def _():
        ext_ref[:, 0:CONV_TAIL, :] = ext_ref[:, q:q + CONV_TAIL, :]

    row = lax.broadcasted_iota(jnp.int32, (q, q), 0)
    colm = lax.broadcasted_iota(jnp.int32, (q, q), 1)
    causal = row >= colm
    tri = causal.astype(BF16)
    lane = lax.broadcasted_iota(jnp.int32, (q, LANES), 1)
    left = lane < SSD_HEAD_DIM

    h_next = _rms(x_ref[...].reshape(n_seq * q, D_MODEL), nmw_ref[...]).astype(BF16)
    n_pieces = -(-SSD_PROJ // SSD_PROJ_PIECE)

    def project(j):
        cols = slice(j * SSD_PROJ_PIECE, min((j + 1) * SSD_PROJ_PIECE, SSD_PROJ))
        proj_ref[:, cols] = _dot(h_next, w_ref[:, cols])

    xs, xdt, xw, b_all, c_all, acum, acum_t, dfs_w, cd_w = ([None] * n_seq for _ in range(9))
    u_cur, dtr = [None] * n_seq, [None] * n_seq
    for i in seqs:
        rows = slice(i * q, (i + 1) * q)
        zs_ref[i] = _silu(proj_ref[rows, 0:SSD_DIM])
        u_cur[i] = proj_ref[rows, SSD_DIM:SSD_DIM + SSD_CONV_CH].astype(BF16)
        ext_ref[i, CONV_TAIL:CONV_TAIL + q, :] = u_cur[i]
        dtr[i] = proj_ref[rows, SSD_DIM + SSD_CONV_CH:] + dtb_ref[...]

    for i in seqs:
        shifted = _dot(shift_ref[...], ext_ref[i])
        project(2 * i)
        acc = convb_ref[...] + convw_ref[SSD_CONV - 1:SSD_CONV, :] * u_cur[i].astype(F32)
        for k in range(SSD_CONV - 1):
            acc = acc + convw_ref[k:k + 1, :] * shifted[k * q:(k + 1) * q]
        xbc = _silu(acc)
        xs[i] = xbc[:, :SSD_DIM]
        b_all[i] = xbc[:, SSD_DIM:SSD_DIM + SSD_GROUPS * SSD_STATE]
        c_all[i] = xbc[:, SSD_DIM + SSD_GROUPS * SSD_STATE:]

        dt = jnp.maximum(dtr[i], 0.0) + jnp.log(1.0 + jnp.exp(-jnp.abs(dtr[i])))
        da = dt * (-jnp.exp(alog_ref[...]) * LOG2E)
        da_hi, da_lo = _split_hi_lo(da)
        da_lo2 = (da - da_hi.astype(F32) - da_lo.astype(F32)).astype(BF16)
        acum[i] = _dot(tri, da_hi) + _dot(tri, da_lo) + _dot(tri, da_lo2)
        project(2 * i + 1)
        acum_t[i] = acum[i].T
        a_last = acum[i][q - 1:q, :]
        dec_from_start = jnp.exp2(acum[i])
        dec_to_end = jnp.exp2(a_last - acum[i])
        chunk_decay = jnp.broadcast_to(jnp.exp2(a_last), (BF16_SUBLANES, LANES))

        stack = jnp.concatenate([dt, dec_from_start, dec_to_end], axis=0).astype(BF16)
        wide = _dot(stack, expand_ref[...])
        cd_hi, cd_lo = _split_hi_lo(chunk_decay)
        cd_w[i] = (_dot(cd_hi, expand_ref[...]) + _dot(cd_lo, expand_ref[...]))[0:1]
        dfs_w[i] = wide[q:2 * q]
        xdt[i] = xs[i] * wide[0:q]
        xw[i] = (xdt[i] * wide[2 * q:3 * q]).astype(BF16)

    for g in range(SSD_GROUPS):
        gsl = slice(g * gw, (g + 1) * gw)
        for i in seqs:
            b_g = b_all[i][:, g * SSD_STATE:(g + 1) * SSD_STATE].astype(BF16)
            c_g = c_all[i][:, g * SSD_STATE:(g + 1) * SSD_STATE].astype(BF16)
            cb = _dot_nt(c_g, b_g)
            s_in = state_ref[i, g]
            y_ref[i, :, gsl] = _dot(c_g, s_in.astype(BF16)) * dfs_w[i][:, gsl]
            state_ref[i, g] = s_in * cd_w[i][:, gsl] + _dot_tn(b_g, xw[i][:, gsl])
            for pair in range(heads_per_group // 2):
                h0 = g * heads_per_group + 2 * pair
                ms = []
                for h in (h0, h0 + 1):
                    seg = acum[i][:, h:h + 1] - acum_t[i][h:h + 1, :]
                    l_dec = jnp.exp2(jnp.where(causal, seg, -jnp.inf))
                    ms.append((cb * l_dec).astype(BF16))
                lhs = jnp.concatenate(ms, axis=1)
                psl = slice(h0 * SSD_HEAD_DIM, (h0 + 2) * SSD_HEAD_DIM)
                xp = xdt[i][:, psl]
                rhs = jnp.concatenate([jnp.where(left, xp, 0.0), jnp.where(left, 0.0, xp)],
                                      axis=0).astype(BF16)
                y_ref[i, :, psl] += _dot(lhs, rhs)

    for j in range(2 * n_seq, n_pieces):
        project(j)
    for i in seqs:
        yz = (y_ref[i] + dskip_ref[...] * xs[i]) * zs_ref[i]
        for g in range(SSD_GROUPS):
            gsl = slice(g * gw, (g + 1) * gw)
            o_ref[i, :, gsl] = _rms(yz[:, gsl], nw_ref[:, gsl]).astype(o_ref.dtype)


def _conv_shift_matrix(q):
    s = np.zeros(((SSD_CONV - 1) * q, CONV_TAIL + q), np.float32)
    for k in range(SSD_CONV - 1):
        for t in range(q):
            s[k * q + t, CONV_TAIL + t - (SSD_CONV - 1) + k] = 1.0
    return jnp.asarray(s, BF16)


def _lookahead_specs(n_seq, n_chunks, rows, width_in, width_out):
    x_spec = pl.BlockSpec((n_seq, rows, width_in),
                          lambda s: (0, jnp.minimum(s, n_chunks - 1), 0))
    o_spec = pl.BlockSpec((n_seq, rows, width_out), lambda s: (0, jnp.maximum(s - 1, 0), 0))
    return x_spec, o_spec


def _ssd(x, norm_mix_w, w_ssd, conv_w, conv_b, dt_bias, a_log, d_skip_w, norm_w, expand):
    n_seq, seqlen, d = x.shape
    q = SSD_CHUNK
    n_chunks = seqlen // q
    shift = _conv_shift_matrix(q)
    x_spec, o_spec = _lookahead_specs(n_seq, n_chunks, q, d, SSD_DIM)
    consts = (norm_mix_w, w_ssd, conv_w, conv_b, dt_bias, a_log, d_skip_w, norm_w, expand, shift)
    return pl.pallas_call(
        _ssd_kernel,
        grid=(n_chunks + 1,),
        in_specs=[x_spec] + [_const_spec(c.shape) for c in consts],
        out_specs=o_spec,
        out_shape=jax.ShapeDtypeStruct((n_seq, seqlen, SSD_DIM), BF16),
        scratch_shapes=[pltpu.VMEM((n_seq * q, SSD_PROJ), F32),
                        pltpu.VMEM((n_seq, CONV_TAIL + q, SSD_CONV_CH), BF16),
                        pltpu.VMEM((n_seq, SSD_GROUPS, SSD_STATE, SSD_DIM // SSD_GROUPS), F32),
                        pltpu.VMEM((n_seq, q, SSD_DIM), F32),
                        pltpu.VMEM((n_seq, q, SSD_DIM), F32)],
        compiler_params=_cparams(("arbitrary",)),
        name="ssd",
    )(x, *consts)


HG_PROJ_PIECES = 8
HG_VPU_MIN_HALF = 8


def _hgrn2_levels(c):
    levels = []
    s = c // 2
    while s >= 1:
        levels.append(s)
        s //= 2
    return levels


def _hgrn2_tables(c):
    idx = np.arange(c)
    blocks = [np.tril(np.ones((c, c), np.float32))]
    masks = []
    for s in _hgrn2_levels(c):
        mid = (idx // (2 * s)) * (2 * s) + s
        upper = idx >= mid
        if 2 < s < HG_VPU_MIN_HALF:
            a = np.zeros((c, c), np.float32)
            for i in range(c):
                if upper[i]:
                    a[i, mid[i]:i + 1] = 1.0
                else:
                    a[i, i + 1:mid[i]] = 1.0
            blocks.append(a)
        same = (idx[:, None] // (2 * s)) == (idx[None, :] // (2 * s))
        masks.append((same & upper[:, None] & (~upper)[None, :]).astype(np.float32))
    masks.append(np.eye(c, dtype=np.float32))
    sums = np.concatenate(blocks, axis=0)
    sums2 = np.concatenate([sums, sums], axis=1)
    masks = np.stack(masks)
    masks2 = np.concatenate([masks, masks], axis=2)
    return jnp.asarray(sums2, BF16), jnp.asarray(masks2, F32)


def _hgrn2_kernel(x_ref, nmw_ref, w_ref, lbp_ref, nw_ref, sums_ref, mask_ref, o_ref,
                  proj_ref, state_ref, g2_ref, q_ref, k_ref, b_ref, ql_ref, klt_ref, qe_ref,
                  ke_ref, v_ref, gate_ref):
    n_seq, c = x_ref.shape[0], x_ref.shape[1]
    levels = _hgrn2_levels(c)
    n_levels = len(levels)
    step = pl.program_id(0)
    seqs = range(n_seq)

    @pl.when(step == 0)
    def _():
        proj_ref[...] = jnp.zeros_like(proj_ref)

    @pl.when(step <= 1)
    def _():
        state_ref[...] = jnp.zeros_like(state_ref)

    lbp = lbp_ref[...]
    e = jnp.exp(lbp - jnp.max(lbp, axis=0, keepdims=True))
    lb = e[0:1, :] / jnp.sum(e, axis=0, keepdims=True)

    for i in seqs:
        rows = slice(i * c, (i + 1) * c)
        q_ref[i] = _silu(proj_ref[rows, 0:HG_KDIM])
        fg = lb + (1.0 - lb) * _sigmoid(proj_ref[rows, HG_KDIM:2 * HG_KDIM])
        v_ref[i] = proj_ref[rows, 2 * HG_KDIM:2 * HG_KDIM + HG_VDIM].astype(BF16)
        gate_ref[i] = _silu(proj_ref[rows, 2 * HG_KDIM + HG_VDIM:])
        g_hi, g_lo = _split_hi_lo(jnp.log2(fg))
        g2_ref[i, 0:c, :] = g_hi
        g2_ref[i, c:2 * c, :] = g_lo
        k_ref[i] = 1.0 - fg
        ql_ref[i, n_levels - 1] = (q_ref[i] * fg).astype(BF16)
        r4 = lax.broadcasted_iota(jnp.int32, fg.shape, 0) % 4
        f_dn = pltpu.roll(fg, 1, axis=0)
        f_up = pltpu.roll(fg, c - 1, axis=0)
        ql_ref[i, n_levels - 2] = (q_ref[i] * jnp.where(r4 == 3, f_dn * fg, fg)).astype(BF16)
        klt_ref[i, n_levels - 2] = (k_ref[i] * jnp.where(r4 == 0, f_up, 1.0)).T

    h_next = _rms(x_ref[...].reshape(n_seq * c, D_MODEL), nmw_ref[...]).astype(BF16)
    piece = HG_PROJ // HG_PROJ_PIECES

    def project(j):
        cols = slice(j * piece, (j + 1) * piece)
        proj_ref[:, cols] = _dot(h_next, w_ref[:, cols])

    fine, state_decay = [None] * n_seq, [None] * n_seq

    def cumsums(i):
        fine[i] = _dot(sums_ref[...], g2_ref[i])
        b_ref[i] = fine[i][0:c]
        b_last = b_ref[i, c - 1:c, :]
        qe_ref[i] = (q_ref[i] * jnp.exp2(b_ref[i])).astype(BF16)
        ke_ref[i] = (k_ref[i] * jnp.exp2(b_last - b_ref[i])).astype(BF16)
        state_decay[i] = jnp.exp2(b_last)

    def level(i, lvl):
        if lvl >= n_levels - 2 and lvl < n_levels:
            return
        if lvl == n_levels:
            ql, kl = q_ref[i], k_ref[i]
        elif levels[lvl] >= HG_VPU_MIN_HALF:
            s = levels[lvl]
            q_rows, k_rows = [], []
            for r0 in range(0, c, 2 * s):
                lo = slice(r0, r0 + s)
                up = slice(r0 + s, r0 + 2 * s)
                ref = b_ref[i, r0 + s - 1:r0 + s, :]
                zeros = jnp.zeros((s, HG_KDIM), F32)
                q_rows += [zeros, q_ref[i, up, :] * jnp.exp2(b_ref[i, up, :] - ref)]
                k_rows += [k_ref[i, lo, :] * jnp.exp2(ref - b_ref[i, lo, :]), zeros]
            ql = jnp.concatenate(q_rows, axis=0)
            kl = jnp.concatenate(k_rows, axis=0)
        else:
            n_fine = 1 + sum(1 for s in levels[:lvl] if 2 < s < HG_VPU_MIN_HALF)
            ex = jnp.exp2(fine[i][n_fine * c:(n_fine + 1) * c])
            ql = q_ref[i] * ex
            kl = k_ref[i] * ex
        ql_ref[i, lvl] = ql.astype(BF16)
        klt_ref[i, lvl] = kl.T

    zero = jnp.zeros((HG_K, c), BF16)

    def head_pair(i, pair):
        psl = slice(2 * pair * HG_K, (2 * pair + 2) * HG_K)
        def block_diag(lvl):
            kt = klt_ref[i, lvl, psl, :].astype(BF16)
            return jnp.concatenate(
                [jnp.concatenate([kt[:HG_K], zero], axis=1),
                 jnp.concatenate([zero, kt[HG_K:]], axis=1)], axis=0)

        att = _dot(ql_ref[i, 0, :, psl], block_diag(0))
        for lvl in range(1, n_levels - 1):
            att = att + _dot(ql_ref[i, lvl, :, psl], block_diag(lvl)) * mask_ref[lvl]
        kbd = block_diag(n_levels)
        for lvl in (n_levels - 1, n_levels):
            att = att + _dot(ql_ref[i, lvl, :, psl], kbd) * mask_ref[lvl]
        for sub in range(2):
            h = 2 * pair + sub
            sl = slice(h * HG_K, (h + 1) * HG_K)
            vh = v_ref[i, :, sl]
            s_t = state_ref[i, h]
            o = _dot_nt(qe_ref[i, :, sl], s_t.astype(BF16))
            o = o + _dot(att[:, sub * c:(sub + 1) * c].astype(BF16), vh)
            state_ref[i, h] = s_t * state_decay[i][:, sl] + _dot_tn(vh, ke_ref[i, :, sl])
            o_ref[i, :, sl] = (_rms(o, nw_ref[...]) * gate_ref[i, :, sl]).astype(o_ref.dtype)

    project(0)
    project(1)
    for i in seqs:
        cumsums(i)
    n_pairs = HG_HEADS // 2
    per_pair = -(-(n_levels + 1) // n_pairs)
    for lvl in range(n_levels + 1):
        level(0, lvl)
        if 2 + lvl < HG_PROJ_PIECES:
            project(2 + lvl)
    for i in seqs:
        for pair in range(n_pairs):
            head_pair(i, pair)
            if i + 1 < n_seq:
                for lvl in range(pair * per_pair, min((pair + 1) * per_pair, n_levels + 1)):
                    level(i + 1, lvl)


def _hgrn2(x, norm_mix_w, w_hg, lb_params, norm_w):
    n_seq, seqlen, d = x.shape
    c = HG_CHUNK
    n_chunks = seqlen // c
    n_levels = len(_hgrn2_levels(c))
    sums, masks = _hgrn2_tables(c)
    consts = (norm_mix_w, w_hg, lb_params, norm_w, sums, masks)
    x_spec, o_spec = _lookahead_specs(n_seq, n_chunks, c, d, HG_VDIM)
    wide = lambda dt, *lead: pltpu.VMEM((n_seq, *lead, c, HG_KDIM), dt)
    return pl.pallas_call(
        _hgrn2_kernel,
        grid=(n_chunks + 1,),
        in_specs=[x_spec] + [_const_spec(a.shape) for a in consts],
        out_specs=o_spec,
        out_shape=jax.ShapeDtypeStruct((n_seq, seqlen, HG_VDIM), BF16),
        scratch_shapes=[pltpu.VMEM((n_seq * c, HG_PROJ), F32),
                        pltpu.VMEM((n_seq, HG_HEADS, HG_V, HG_K), F32),
                        pltpu.VMEM((n_seq, 2 * c, HG_KDIM), BF16),
                        wide(F32), wide(F32), wide(F32),
                        wide(BF16, n_levels + 1),
                        pltpu.VMEM((n_seq, n_levels + 1, HG_KDIM, c), F32),
                        wide(BF16), wide(BF16),
                        wide(BF16), wide(F32)],
        compiler_params=_cparams(("arbitrary",)),
        name="hgrn2",
    )(x, *consts)


def _attn_kernel(x_ref, ya_ref, ob_ref, wout_ref, nw_ref, wqk_ref, wvo_ref, o_ref, p_ref):
    x1 = x_ref[...] + _dot(ya_ref[...], wout_ref[0:SSD_DIM, :]) \
        + _dot(ob_ref[...], wout_ref[SSD_DIM:, :])
    h = _rms(x1, nw_ref[...]).astype(BF16)
    scores = _dot(h, wqk_ref[0])
    mlen = wqk_ref.shape[2] // XA_HEADS
    for hd in range(XA_HEADS):
        ml = slice(hd * mlen, (hd + 1) * mlen)
        sc = scores[:, ml]
        p = jnp.exp(sc - jnp.max(sc, axis=-1, keepdims=True))
        p_ref[:, ml] = (p / jnp.sum(p, axis=-1, keepdims=True)).astype(BF16)
    o_ref[...] = x1 + _dot(p_ref[...], wvo_ref[0])


def _attn(x2d, ya, ob, w_out, norm_w, wqk, wvo, seqlen):
    t, d = x2d.shape
    tm = min(ATTN_ROWS, seqlen)
    tpb = seqlen // tm
    n_mem = wqk.shape[2]
    row = lambda n: pl.BlockSpec((tm, n), lambda i: (i, 0))
    return pl.pallas_call(
        _attn_kernel,
        grid=(t // tm,),
        in_specs=[row(d), row(SSD_DIM), row(HG_VDIM), _const_spec(w_out.shape),
                  _const_spec(norm_w.shape),
                  pl.BlockSpec((1, d, n_mem), lambda i: (i // tpb, 0, 0)),
                  pl.BlockSpec((1, n_mem, d), lambda i: (i // tpb, 0, 0))],
        out_specs=row(d),
        out_shape=jax.ShapeDtypeStruct((t, d), F32),
        scratch_shapes=[pltpu.VMEM((tm, n_mem), BF16)],
        compiler_params=_cparams(("arbitrary",)),
        name="attn",
    )(x2d, ya, ob, w_out, norm_w, wqk, wvo)


def _ffn_kernel(x_ref, nw_ref, wg_ref, wu_ref, wd_ref, nf_ref, o_ref, acc_ref):
    x = x_ref[...]
    h = _rms(x, nw_ref[...]).astype(BF16)
    acc_ref[...] = x
    n_cols = wg_ref.shape[1]
    for c0 in range(0, n_cols, FFN_COLS):
        g = _dot(h, wg_ref[:, c0:c0 + FFN_COLS])
        u = _dot(h, wu_ref[:, c0:c0 + FFN_COLS])
        a = (_silu(g) * u).astype(BF16)
        acc_ref[...] += _dot(a, wd_ref[c0:c0 + FFN_COLS, :])
    o_ref[...] = _rms(acc_ref[...], nf_ref[...])


def _ffn(x2d, norm_w, wg, wu, wd, norm_final):
    t, d = x2d.shape
    tm = min(FFN_ROWS, t)
    row = pl.BlockSpec((tm, d), lambda i: (i, 0))
    return pl.pallas_call(
        _ffn_kernel,
        grid=(t // tm,),
        in_specs=[row, _const_spec(norm_w.shape), _const_spec(wg.shape), _const_spec(wu.shape),
                  _const_spec(wd.shape), _const_spec(norm_final.shape)],
        out_specs=row,
        out_shape=jax.ShapeDtypeStruct((t, d), F32),
        scratch_shapes=[pltpu.VMEM((tm, d), F32)],
        compiler_params=_cparams(("arbitrary",)),
        name="ffn",
    )(x2d, norm_w, wg, wu, wd, norm_final)


def _pad_lanes(v):
    return jnp.pad(v.astype(F32), (0, LANES - v.shape[0])).reshape(1, LANES)


def kernel(x, mem, norm_mix_w, w_in, conv_w, conv_b, dt_bias, a_log, d_skip, ssd_norm_w,
           hg_lower_bounds, hg_norm_w, w_out, norm_xa_w, norm_mem_w, xa_wq, xa_wkv, xa_wo,
           norm_ffn_w, ffn_w_gate, ffn_w_up, ffn_w_down, norm_final_w):
    bsz, seqlen, d = x.shape
    assert d == D_MODEL and norm_mix_w.shape[0] == 1, "single-layer block of width 1024"
    assert hg_lower_bounds.shape[0] == 2
    assert seqlen % SSD_CHUNK == 0 and seqlen % HG_CHUNK == 0
    t = bsz * seqlen
    x2d = x.reshape(t, d)
    row = lambda v: v.reshape(1, -1).astype(F32)

    s3 = SSD_DIM + SSD_CONV_CH + SSD_HEADS
    w = w_in[0]
    w_ssd = jnp.pad(w[:, :s3], ((0, 0), (0, LANES - SSD_HEADS))).astype(BF16)
    w_hg = w[:, s3:].astype(BF16)

    expand = (np.arange(LANES)[:, None] == (np.arange(SSD_DIM)[None, :] // SSD_HEAD_DIM))
    expand = jnp.asarray(expand, BF16)
    d_skip_w = jnp.repeat(d_skip[0].astype(F32), SSD_HEAD_DIM).reshape(1, SSD_DIM)
    y_a = _ssd(x, row(norm_mix_w[0]), w_ssd, conv_w[0].astype(F32), row(conv_b[0]),
               _pad_lanes(dt_bias[0]), _pad_lanes(a_log[0]), d_skip_w, row(ssd_norm_w[0]),
               expand).reshape(t, SSD_DIM)
    o_b = _hgrn2(x, row(norm_mix_w[0]), w_hg, hg_lower_bounds.astype(F32),
                 row(hg_norm_w[0])).reshape(t, HG_VDIM)

    wqk, wvo = _mem_kv(mem, row(norm_mem_w[0]), xa_wkv[0].astype(BF16), xa_wq[0].astype(BF16),
                       xa_wo[0].astype(BF16))
    x2 = _attn(x2d, y_a, o_b, w_out[0].astype(BF16), row(norm_xa_w[0]), wqk, wvo, seqlen)
    out = _ffn(x2, row(norm_ffn_w[0]), ffn_w_gate[0].astype(BF16), ffn_w_up[0].astype(BF16),
               ffn_w_down[0].astype(BF16), row(norm_final_w))
    return out.reshape(bsz, seqlen, d)
```

```python
import math

import numpy as np
import jax
import jax.numpy as jnp
from jax import lax
from jax.experimental import pallas as pl
from jax.experimental.pallas import tpu as pltpu

F32 = jnp.float32
BF16 = jnp.bfloat16
EPS = 1e-6
LOG2E = math.log2(math.e)

D_MODEL = 1024
SSD_HEADS = 16
SSD_HEAD_DIM = 64
SSD_DIM = SSD_HEADS * SSD_HEAD_DIM
SSD_GROUPS = 2
SSD_STATE = 128
SSD_CONV = 4
SSD_CHUNK = 128
SSD_CONV_CH = SSD_DIM + 2 * SSD_GROUPS * SSD_STATE
HG_HEADS = 8
HG_K = 128
HG_V = 128
HG_KDIM = HG_HEADS * HG_K
HG_VDIM = HG_HEADS * HG_V
XA_HEADS = 4
XA_HEAD_DIM = D_MODEL // XA_HEADS

LANES = 128
BF16_SUBLANES = 16
CONV_TAIL = BF16_SUBLANES
VMEM_LIMIT_BYTES = 56 * 1024 * 1024

HG_CHUNK = 128
ATTN_ROWS = 1024
FFN_ROWS = 512
FFN_COLS = 256

SSD_PROJ = SSD_DIM + SSD_CONV_CH + LANES
SSD_PROJ_PIECE = 768
HG_PROJ = 2 * HG_KDIM + 2 * HG_VDIM


def _cparams(sem):
    return pltpu.CompilerParams(dimension_semantics=sem, vmem_limit_bytes=VMEM_LIMIT_BYTES)


def _const_spec(shape):
    zeros = (0,) * len(shape)
    return pl.BlockSpec(shape, lambda *_: zeros)


def _dot(a, b):
    return jnp.dot(a, b, preferred_element_type=F32)


def _dot_nt(a, b):
    return lax.dot_general(a, b, (((1,), (1,)), ((), ())), preferred_element_type=F32)


def _dot_tn(a, b):
    return lax.dot_general(a, b, (((0,), (0,)), ((), ())), preferred_element_type=F32)


def _sigmoid(x):
    return 1.0 / (1.0 + jnp.exp(-x))


def _silu(x):
    return x * _sigmoid(x)


def _rms(x, w):
    ms = jnp.mean(x * x, axis=-1, keepdims=True)
    return x * lax.rsqrt(ms + EPS) * w


def _split_hi_lo(x):
    hi = x.astype(BF16)
    lo = (x - hi.astype(F32)).astype(BF16)
    return hi, lo


def _mem_kv_kernel(mem_ref, nw_ref, wkv_ref, wq_ref, wo_ref, wqk_ref, wvo_ref):
    m = _rms(mem_ref[0], nw_ref[...]).astype(BF16)
    kv = _dot(m, wkv_ref[...])
    km = kv[:, :D_MODEL].astype(BF16)
    vm = kv[:, D_MODEL:].astype(BF16)
    mlen = km.shape[0]
    for hd in range(XA_HEADS):
        sl = slice(hd * XA_HEAD_DIM, (hd + 1) * XA_HEAD_DIM)
        ml = slice(hd * mlen, (hd + 1) * mlen)
        wqk_ref[0, :, ml] = (_dot_nt(wq_ref[:, sl], km[:, sl]) * (XA_HEAD_DIM ** -0.5)).astype(BF16)
        wvo_ref[0, ml, :] = _dot(vm[:, sl], wo_ref[sl, :]).astype(BF16)


def _mem_kv(mem, norm_w, wkv, wq, wo):
    bsz, mlen, d = mem.shape
    return pl.pallas_call(
        _mem_kv_kernel,
        grid=(bsz,),
        in_specs=[pl.BlockSpec((1, mlen, d), lambda b: (b, 0, 0)),
                  _const_spec((1, d)), _const_spec((d, 2 * d)),
                  _const_spec(wq.shape), _const_spec(wo.shape)],
        out_specs=[pl.BlockSpec((1, d, XA_HEADS * mlen), lambda b: (b, 0, 0)),
                   pl.BlockSpec((1, XA_HEADS * mlen, d), lambda b: (b, 0, 0))],
        out_shape=[jax.ShapeDtypeStruct((bsz, d, XA_HEADS * mlen), BF16),
                   jax.ShapeDtypeStruct((bsz, XA_HEADS * mlen, d), BF16)],
        compiler_params=_cparams(("arbitrary",)),
        name="mem_kv",
    )(mem, norm_w, wkv, wq, wo)


def _ssd_kernel(x_ref, nmw_ref, w_ref, convw_ref, convb_ref, dtb_ref, alog_ref,
                dskip_ref, nw_ref, expand_ref, shift_ref, o_ref,
                proj_ref, ext_ref, state_ref, y_ref, zs_ref):
    n_seq, q = x_ref.shape[0], x_ref.shape[1]
    gw = SSD_DIM // SSD_GROUPS
    heads_per_group = SSD_HEADS // SSD_GROUPS
    step = pl.program_id(0)
    seqs = range(n_seq)

    @pl.when(step == 0)
    def _():
        proj_ref[...] = jnp.zeros_like(proj_ref)

    @pl.when(step <= 1)
    def _():
        ext_ref[:, 0:CONV_TAIL, :] = jnp.zeros((n_seq, CONV_TAIL, SSD_CONV_CH), BF16)
        state_ref[...] = jnp.zeros_like(state_ref)

    @pl.when(step > 1)
    def _():
        ext_ref[:, 0:CONV_TAIL, :] = ext_ref[:, q:q + CONV_TAIL, :]

    row = lax.broadcasted_iota(jnp.int32, (q, q), 0)
    colm = lax.broadcasted_iota(jnp.int32, (q, q), 1)
    causal = row >= colm
    tri = causal.astype(BF16)
    lane = lax.broadcasted_iota(jnp.int32, (q, LANES), 1)
    left = lane < SSD_HEAD_DIM

    h_next = _rms(x_ref[...].reshape(n_seq * q, D_MODEL), nmw_ref[...]).astype(BF16)
    n_pieces = -(-SSD_PROJ // SSD_PROJ_PIECE)

    def project(j):
        cols = slice(j * SSD_PROJ_PIECE, min((j + 1) * SSD_PROJ_PIECE, SSD_PROJ))
        proj_ref[:, cols] = _dot(h_next, w_ref[:, cols])

    xs, xdt, xw, b_all, c_all, acum, acum_t, dfs_w, cd_w = ([None] * n_seq for _ in range(9))
    u_cur, dtr = [None] * n_seq, [None] * n_seq
    for i in seqs:
        rows = slice(i * q, (i + 1) * q)
        zs_ref[i] = _silu(proj_ref[rows, 0:SSD_DIM])
        u_cur[i] = proj_ref[rows, SSD_DIM:SSD_DIM + SSD_CONV_CH].astype(BF16)
        ext_ref[i, CONV_TAIL:CONV_TAIL + q, :] = u_cur[i]
        dtr[i] = proj_ref[rows, SSD_DIM + SSD_CONV_CH:] + dtb_ref[...]

    for i in seqs:
        shifted = _dot(shift_ref[...], ext_ref[i])
        project(2 * i)
        acc = convb_ref[...] + convw_ref[SSD_CONV - 1:SSD_CONV, :] * u_cur[i].astype(F32)
        for k in range(SSD_CONV - 1):
            acc = acc + convw_ref[k:k + 1, :] * shifted[k * q:(k + 1) * q]
        xbc = _silu(acc)
        xs[i] = xbc[:, :SSD_DIM]
        b_all[i] = xbc[:, SSD_DIM:SSD_DIM + SSD_GROUPS * SSD_STATE]
        c_all[i] = xbc[:, SSD_DIM + SSD_GROUPS * SSD_STATE:]

        dt = jnp.maximum(dtr[i], 0.0) + jnp.log(1.0 + jnp.exp(-jnp.abs(dtr[i])))
        da = dt * (-jnp.exp(alog_ref[...]) * LOG2E)
        da_hi, da_lo = _split_hi_lo(da)
        da_lo2 = (da - da_hi.astype(F32) - da_lo.astype(F32)).astype(BF16)
        acum[i] = _dot(tri, da_hi) + _dot(tri, da_lo) + _dot(tri, da_lo2)
        project(2 * i + 1)
        acum_t[i] = acum[i].T
        a_last = acum[i][q - 1:q, :]
        dec_from_start = jnp.exp2(acum[i])
        dec_to_end = jnp.exp2(a_last - acum[i])
        chunk_decay = jnp.broadcast_to(jnp.exp2(a_last), (BF16_SUBLANES, LANES))

        stack = jnp.concatenate([dt, dec_from_start, dec_to_end], axis=0).astype(BF16)
        wide = _dot(stack, expand_ref[...])
        cd_hi, cd_lo = _split_hi_lo(chunk_decay)
        cd_w[i] = (_dot(cd_hi, expand_ref[...]) + _dot(cd_lo, expand_ref[...]))[0:1]
        dfs_w[i] = wide[q:2 * q]
        xdt[i] = xs[i] * wide[0:q]
        xw[i] = (xdt[i] * wide[2 * q:3 * q]).astype(BF16)

    for g in range(SSD_GROUPS):
        gsl = slice(g * gw, (g + 1) * gw)
        for i in seqs:
            b_g = b_all[i][:, g * SSD_STATE:(g + 1) * SSD_STATE].astype(BF16)
            c_g = c_all[i][:, g * SSD_STATE:(g + 1) * SSD_STATE].astype(BF16)
            cb = _dot_nt(c_g, b_g)
            s_in = state_ref[i, g]
            y_ref[i, :, gsl] = _dot(c_g, s_in.astype(BF16)) * dfs_w[i][:, gsl]
            state_ref[i, g] = s_in * cd_w[i][:, gsl] + _dot_tn(b_g, xw[i][:, gsl])
            for pair in range(heads_per_group // 2):
                h0 = g * heads_per_group + 2 * pair
                ms = []
                for h in (h0, h0 + 1):
                    seg = acum[i][:, h:h + 1] - acum_t[i][h:h + 1, :]
                    l_dec = jnp.exp2(jnp.where(causal, seg, -jnp.inf))
                    ms.append((cb * l_dec).astype(BF16))
                lhs = jnp.concatenate(ms, axis=1)
                psl = slice(h0 * SSD_HEAD_DIM, (h0 + 2) * SSD_HEAD_DIM)
                xp = xdt[i][:, psl]
                rhs = jnp.concatenate([jnp.where(left, xp, 0.0), jnp.where(left, 0.0, xp)],
                                      axis=0).astype(BF16)
                y_ref[i, :, psl] += _dot(lhs, rhs)

    for j in range(2 * n_seq, n_pieces):
        project(j)
    for i in seqs:
        yz = (y_ref[i] + dskip_ref[...] * xs[i]) * zs_ref[i]
        for g in range(SSD_GROUPS):
            gsl = slice(g * gw, (g + 1) * gw)
            o_ref[i, :, gsl] = _rms(yz[:, gsl], nw_ref[:, gsl]).astype(o_ref.dtype)


def _conv_shift_matrix(q):
    s = np.zeros(((SSD_CONV - 1) * q, CONV_TAIL + q), np.float32)
    for k in range(SSD_CONV - 1):
        for t in range(q):
            s[k * q + t, CONV_TAIL + t - (SSD_CONV - 1) + k] = 1.0
    return jnp.asarray(s, BF16)


def _lookahead_specs(n_seq, n_chunks, rows, width_in, width_out):
    x_spec = pl.BlockSpec((n_seq, rows, width_in),
                          lambda s: (0, jnp.minimum(s, n_chunks - 1), 0))
    o_spec = pl.BlockSpec((n_seq, rows, width_out), lambda s: (0, jnp.maximum(s - 1, 0), 0))
    return x_spec, o_spec


def _ssd(x, norm_mix_w, w_ssd, conv_w, conv_b, dt_bias, a_log, d_skip_w, norm_w, expand):
    n_seq, seqlen, d = x.shape
    q = SSD_CHUNK
    n_chunks = seqlen // q
    shift = _conv_shift_matrix(q)
    x_spec, o_spec = _lookahead_specs(n_seq, n_chunks, q, d, SSD_DIM)
    consts = (norm_mix_w, w_ssd, conv_w, conv_b, dt_bias, a_log, d_skip_w, norm_w, expand, shift)
    return pl.pallas_call(
        _ssd_kernel,
        grid=(n_chunks + 1,),
        in_specs=[x_spec] + [_const_spec(c.shape) for c in consts],
        out_specs=o_spec,
        out_shape=jax.ShapeDtypeStruct((n_seq, seqlen, SSD_DIM), BF16),
        scratch_shapes=[pltpu.VMEM((n_seq * q, SSD_PROJ), F32),
                        pltpu.VMEM((n_seq, CONV_TAIL + q, SSD_CONV_CH), BF16),
                        pltpu.VMEM((n_seq, SSD_GROUPS, SSD_STATE, SSD_DIM // SSD_GROUPS), F32),
                        pltpu.VMEM((n_seq, q, SSD_DIM), F32),
                        pltpu.VMEM((n_seq, q, SSD_DIM), F32)],
        compiler_params=_cparams(("arbitrary",)),
        name="ssd",
    )(x, *consts)


HG_PROJ_PIECES = 8
HG_VPU_MIN_HALF = 8


def _hgrn2_levels(c):
    levels = []
    s = c // 2
    while s >= 1:
        levels.append(s)
        s //= 2
    return levels


def _hgrn2_tables(c):
    idx = np.arange(c)
    blocks = [np.tril(np.ones((c, c), np.float32))]
    masks = []
    for s in _hgrn2_levels(c):
        mid = (idx // (2 * s)) * (2 * s) + s
        upper = idx >= mid
        if 2 < s < HG_VPU_MIN_HALF:
            a = np.zeros((c, c), np.float32)
            for i in range(c):
                if upper[i]:
                    a[i, mid[i]:i + 1] = 1.0
                else:
                    a[i, i + 1:mid[i]] = 1.0
            blocks.append(a)
        same = (idx[:, None] // (2 * s)) == (idx[None, :] // (2 * s))
        masks.append((same & upper[:, None] & (~upper)[None, :]).astype(np.float32))
    masks.append(np.eye(c, dtype=np.float32))
    sums = np.concatenate(blocks, axis=0)
    sums2 = np.concatenate([sums, sums], axis=1)
    masks = np.stack(masks)
    masks2 = np.concatenate([masks, masks], axis=2)
    return jnp.asarray(sums2, BF16), jnp.asarray(masks2, F32)


def _hgrn2_kernel(x_ref, nmw_ref, w_ref, lbp_ref, nw_ref, sums_ref, mask_ref, o_ref,
                  proj_ref, state_ref, g2_ref, q_ref, k_ref, b_ref, ql_ref, klt_ref, qe_ref,
                  ke_ref, v_ref, gate_ref):
    n_seq, c = x_ref.shape[0], x_ref.shape[1]
    levels = _hgrn2_levels(c)
    n_levels = len(levels)
    step = pl.program_id(0)
    seqs = range(n_seq)

    @pl.when(step == 0)
    def _():
        proj_ref[...] = jnp.zeros_like(proj_ref)

    @pl.when(step <= 1)
    def _():
        state_ref[...] = jnp.zeros_like(state_ref)

    lbp = lbp_ref[...]
    e = jnp.exp(lbp - jnp.max(lbp, axis=0, keepdims=True))
    lb = e[0:1, :] / jnp.sum(e, axis=0, keepdims=True)

    for i in seqs:
        rows = slice(i * c, (i + 1) * c)
        q_ref[i] = _silu(proj_ref[rows, 0:HG_KDIM])
        fg = lb + (1.0 - lb) * _sigmoid(proj_ref[rows, HG_KDIM:2 * HG_KDIM])
        v_ref[i] = proj_ref[rows, 2 * HG_KDIM:2 * HG_KDIM + HG_VDIM].astype(BF16)
        gate_ref[i] = _silu(proj_ref[rows, 2 * HG_KDIM + HG_VDIM:])
        g_hi, g_lo = _split_hi_lo(jnp.log2(fg))
        g2_ref[i, 0:c, :] = g_hi
        g2_ref[i, c:2 * c, :] = g_lo
        k_ref[i] = 1.0 - fg
        ql_ref[i, n_levels - 1] = (q_ref[i] * fg).astype(BF16)
        r4 = lax.broadcasted_iota(jnp.int32, fg.shape, 0) % 4
        f_dn = pltpu.roll(fg, 1, axis=0)
        f_up = pltpu.roll(fg, c - 1, axis=0)
        ql_ref[i, n_levels - 2] = (q_ref[i] * jnp.where(r4 == 3, f_dn * fg, fg)).astype(BF16)
        klt_ref[i, n_levels - 2] = (k_ref[i] * jnp.where(r4 == 0, f_up, 1.0)).T

    h_next = _rms(x_ref[...].reshape(n_seq * c, D_MODEL), nmw_ref[...]).astype(BF16)
    piece = HG_PROJ // HG_PROJ_PIECES

    def project(j):
        cols = slice(j * piece, (j + 1) * piece)
        proj_ref[:, cols] = _dot(h_next, w_ref[:, cols])

    fine, state_decay = [None] * n_seq, [None] * n_seq

    def cumsums(i):
        fine[i] = _dot(sums_ref[...], g2_ref[i])
        b_ref[i] = fine[i][0:c]
        b_last = b_ref[i, c - 1:c, :]
        qe_ref[i] = (q_ref[i] * jnp.exp2(b_ref[i])).astype(BF16)
        ke_ref[i] = (k_ref[i] * jnp.exp2(b_last - b_ref[i])).astype(BF16)
        state_decay[i] = jnp.exp2(b_last)

    def level(i, lvl):
        if lvl >= n_levels - 2 and lvl < n_levels:
            return
        if lvl == n_levels:
            ql, kl = q_ref[i], k_ref[i]
        elif levels[lvl] >= HG_VPU_MIN_HALF:
            s = levels[lvl]
            q_rows, k_rows = [], []
            for r0 in range(0, c, 2 * s):
                lo = slice(r0, r0 + s)
                up = slice(r0 + s, r0 + 2 * s)
                ref = b_ref[i, r0 + s - 1:r0 + s, :]
                zeros = jnp.zeros((s, HG_KDIM), F32)
                q_rows += [zeros, q_ref[i, up, :] * jnp.exp2(b_ref[i, up, :] - ref)]
                k_rows += [k_ref[i, lo, :] * jnp.exp2(ref - b_ref[i, lo, :]), zeros]
            ql = jnp.concatenate(q_rows, axis=0)
            kl = jnp.concatenate(k_rows, axis=0)
        else:
            n_fine = 1 + sum(1 for s in levels[:lvl] if 2 < s < HG_VPU_MIN_HALF)
            ex = jnp.exp2(fine[i][n_fine * c:(n_fine + 1) * c])
            ql = q_ref[i] * ex
            kl = k_ref[i] * ex
        ql_ref[i, lvl] = ql.astype(BF16)
        klt_ref[i, lvl] = kl.T

    zero = jnp.zeros((HG_K, c), BF16)

    def head_pair(i, pair):
        psl = slice(2 * pair * HG_K, (2 * pair + 2) * HG_K)
        def block_diag(lvl):
            kt = klt_ref[i, lvl, psl, :].astype(BF16)
            return jnp.concatenate(
                [jnp.concatenate([kt[:HG_K], zero], axis=1),
                 jnp.concatenate([zero, kt[HG_K:]], axis=1)], axis=0)

        att = _dot(ql_ref[i, 0, :, psl], block_diag(0))
        for lvl in range(1, n_levels - 1):
            att = att + _dot(ql_ref[i, lvl, :, psl], block_diag(lvl)) * mask_ref[lvl]
        kbd = block_diag(n_levels)
        for lvl in (n_levels - 1, n_levels):
            att = att + _dot(ql_ref[i, lvl, :, psl], kbd) * mask_ref[lvl]
        for sub in range(2):
            h = 2 * pair + sub
            sl = slice(h * HG_K, (h + 1) * HG_K)
            vh = v_ref[i, :, sl]
            s_t = state_ref[i, h]
            o = _dot_nt(qe_ref[i, :, sl], s_t.astype(BF16))
            o = o + _dot(att[:, sub * c:(sub + 1) * c].astype(BF16), vh)
            state_ref[i, h] = s_t * state_decay[i][:, sl] + _dot_tn(vh, ke_ref[i, :, sl])
            o_ref[i, :, sl] = (_rms(o, nw_ref[...]) * gate_ref[i, :, sl]).astype(o_ref.dtype)

    project(0)
    project(1)
    for i in seqs:
        cumsums(i)
    n_pairs = HG_HEADS // 2
    per_pair = -(-(n_levels + 1) // n_pairs)
    for lvl in range(n_levels + 1):
        level(0, lvl)
        if 2 + lvl < HG_PROJ_PIECES:
            project(2 + lvl)
    for i in seqs:
        for pair in range(n_pairs):
            head_pair(i, pair)
            if i + 1 < n_seq:
                for lvl in range(pair * per_pair, min((pair + 1) * per_pair, n_levels + 1)):
                    level(i + 1, lvl)


def _hgrn2(x, norm_mix_w, w_hg, lb_params, norm_w):
    n_seq, seqlen, d = x.shape
    c = HG_CHUNK
    n_chunks = seqlen // c
    n_levels = len(_hgrn2_levels(c))
    sums, masks = _hgrn2_tables(c)
    consts = (norm_mix_w, w_hg, lb_params, norm_w, sums, masks)
    x_spec, o_spec = _lookahead_specs(n_seq, n_chunks, c, d, HG_VDIM)
    wide = lambda dt, *lead: pltpu.VMEM((n_seq, *lead, c, HG_KDIM), dt)
    return pl.pallas_call(
        _hgrn2_kernel,
        grid=(n_chunks + 1,),
        in_specs=[x_spec] + [_const_spec(a.shape) for a in consts],
        out_specs=o_spec,
        out_shape=jax.ShapeDtypeStruct((n_seq, seqlen, HG_VDIM), BF16),
        scratch_shapes=[pltpu.VMEM((n_seq * c, HG_PROJ), F32),
                        pltpu.VMEM((n_seq, HG_HEADS, HG_V, HG_K), F32),
                        pltpu.VMEM((n_seq, 2 * c, HG_KDIM), BF16),
                        wide(F32), wide(F32), wide(F32),
                        wide(BF16, n_levels + 1),
                        pltpu.VMEM((n_seq, n_levels + 1, HG_KDIM, c), F32),
                        wide(BF16), wide(BF16),
                        wide(BF16), wide(F32)],
        compiler_params=_cparams(("arbitrary",)),
        name="hgrn2",
    )(x, *consts)


def _attn_kernel(x_ref, ya_ref, ob_ref, wout_ref, nw_ref, wqk_ref, wvo_ref, o_ref, p_ref):
    x1 = x_ref[...] + _dot(ya_ref[...], wout_ref[0:SSD_DIM, :]) \
        + _dot(ob_ref[...], wout_ref[SSD_DIM:, :])
    h = _rms(x1, nw_ref[...]).astype(BF16)
    scores = _dot(h, wqk_ref[0])
    mlen = wqk_ref.shape[2] // XA_HEADS
    for hd in range(XA_HEADS):
        ml = slice(hd * mlen, (hd + 1) * mlen)
        sc = scores[:, ml]
        p = jnp.exp(sc - jnp.max(sc, axis=-1, keepdims=True))
        p_ref[:, ml] = (p / jnp.sum(p, axis=-1, keepdims=True)).astype(BF16)
    o_ref[...] = x1 + _dot(p_ref[...], wvo_ref[0])


def _attn(x2d, ya, ob, w_out, norm_w, wqk, wvo, seqlen):
    t, d = x2d.shape
    tm = min(ATTN_ROWS, seqlen)
    tpb = seqlen // tm
    n_mem = wqk.shape[2]
    row = lambda n: pl.BlockSpec((tm, n), lambda i: (i, 0))
    return pl.pallas_call(
        _attn_kernel,
        grid=(t // tm,),
        in_specs=[row(d), row(SSD_DIM), row(HG_VDIM), _const_spec(w_out.shape),
                  _const_spec(norm_w.shape),
                  pl.BlockSpec((1, d, n_mem), lambda i: (i // tpb, 0, 0)),
                  pl.BlockSpec((1, n_mem, d), lambda i: (i // tpb, 0, 0))],
        out_specs=row(d),
        out_shape=jax.ShapeDtypeStruct((t, d), F32),
        scratch_shapes=[pltpu.VMEM((tm, n_mem), BF16)],
        compiler_params=_cparams(("arbitrary",)),
        name="attn",
    )(x2d, ya, ob, w_out, norm_w, wqk, wvo)


def _ffn_kernel(x_ref, nw_ref, wg_ref, wu_ref, wd_ref, nf_ref, o_ref, acc_ref):
    x = x_ref[...]
    h = _rms(x, nw_ref[...]).astype(BF16)
    acc_ref[...] = x
    n_cols = wg_ref.shape[1]
    for c0 in range(0, n_cols, FFN_COLS):
        g = _dot(h, wg_ref[:, c0:c0 + FFN_COLS])
        u = _dot(h, wu_ref[:, c0:c0 + FFN_COLS])
        a = (_silu(g) * u).astype(BF16)
        acc_ref[...] += _dot(a, wd_ref[c0:c0 + FFN_COLS, :])
    o_ref[...] = _rms(acc_ref[...], nf_ref[...])


def _ffn(x2d, norm_w, wg, wu, wd, norm_final):
    t, d = x2d.shape
    tm = min(FFN_ROWS, t)
    row = pl.BlockSpec((tm, d), lambda i: (i, 0))
    return pl.pallas_call(
        _ffn_kernel,
        grid=(t // tm,),
        in_specs=[row, _const_spec(norm_w.shape), _const_spec(wg.shape), _const_spec(wu.shape),
                  _const_spec(wd.shape), _const_spec(norm_final.shape)],
        out_specs=row,
        out_shape=jax.ShapeDtypeStruct((t, d), F32),
        scratch_shapes=[pltpu.VMEM((tm, d), F32)],
        compiler_params=_cparams(("arbitrary",)),
        name="ffn",
    )(x2d, norm_w, wg, wu, wd, norm_final)


def _pad_lanes(v):
    return jnp.pad(v.astype(F32), (0, LANES - v.shape[0])).reshape(1, LANES)


def kernel(x, mem, norm_mix_w, w_in, conv_w, conv_b, dt_bias, a_log, d_skip, ssd_norm_w,
           hg_lower_bounds, hg_norm_w, w_out, norm_xa_w, norm_mem_w, xa_wq, xa_wkv, xa_wo,
           norm_ffn_w, ffn_w_gate, ffn_w_up, ffn_w_down, norm_final_w):
    bsz, seqlen, d = x.shape
    assert d == D_MODEL and norm_mix_w.shape[0] == 1, "single-layer block of width 1024"
    assert hg_lower_bounds.shape[0] == 2
    assert seqlen % SSD_CHUNK == 0 and seqlen % HG_CHUNK == 0
    t = bsz * seqlen
    x2d = x.reshape(t, d)
    row = lambda v: v.reshape(1, -1).astype(F32)

    s3 = SSD_DIM + SSD_CONV_CH + SSD_HEADS
    w = w_in[0]
    w_ssd = jnp.pad(w[:, :s3], ((0, 0), (0, LANES - SSD_HEADS))).astype(BF16)
    w_hg = w[:, s3:].astype(BF16)

    expand = (np.arange(LANES)[:, None] == (np.arange(SSD_DIM)[None, :] // SSD_HEAD_DIM))
    expand = jnp.asarray(expand, BF16)
    d_skip_w = jnp.repeat(d_skip[0].astype(F32), SSD_HEAD_DIM).reshape(1, SSD_DIM)
    y_a = _ssd(x, row(norm_mix_w[0]), w_ssd, conv_w[0].astype(F32), row(conv_b[0]),
               _pad_lanes(dt_bias[0]), _pad_lanes(a_log[0]), d_skip_w, row(ssd_norm_w[0]),
               expand).reshape(t, SSD_DIM)
    o_b = _hgrn2(x, row(norm_mix_w[0]), w_hg, hg_lower_bounds.astype(F32),
                 row(hg_norm_w[0])).reshape(t, HG_VDIM)

    wqk, wvo = _mem_kv(mem, row(norm_mem_w[0]), xa_wkv[0].astype(BF16), xa_wq[0].astype(BF16),
                       xa_wo[0].astype(BF16))
    x2 = _attn(x2d, y_a, o_b, w_out[0].astype(BF16), row(norm_xa_w[0]), wqk, wvo, seqlen)
    out = _ffn(x2, row(norm_ffn_w[0]), ffn_w_gate[0].astype(BF16), ffn_w_up[0].astype(BF16),
               ffn_w_down[0].astype(BF16), row(norm_final_w))
    return out.reshape(bsz, seqlen, d)
```

```python
import math

import numpy as np
import jax
import jax.numpy as jnp
from jax import lax
from jax.experimental import pallas as pl
from jax.experimental.pallas import tpu as pltpu

F32 = jnp.float32
BF16 = jnp.bfloat16
EPS = 1e-6
LOG2E = math.log2(math.e)

D_MODEL = 1024
SSD_HEADS = 16
SSD_HEAD_DIM = 64
SSD_DIM = SSD_HEADS * SSD_HEAD_DIM
SSD_GROUPS = 2
SSD_STATE = 128
SSD_CONV = 4
SSD_CHUNK = 128
SSD_CONV_CH = SSD_DIM + 2 * SSD_GROUPS * SSD_STATE
HG_HEADS = 8
HG_K = 128
HG_V = 128
HG_KDIM = HG_HEADS * HG_K
HG_VDIM = HG_HEADS * HG_V
XA_HEADS = 4
XA_HEAD_DIM = D_MODEL // XA_HEADS

LANES = 128
BF16_SUBLANES = 16
CONV_TAIL = BF16_SUBLANES
VMEM_LIMIT_BYTES = 56 * 1024 * 1024

HG_CHUNK = 128
ATTN_ROWS = 1024
FFN_ROWS = 1024
FFN_COLS = 256

SSD_PROJ = SSD_DIM + SSD_CONV_CH + LANES
SSD_PROJ_PIECE = 768
HG_PROJ = 2 * HG_KDIM + 2 * HG_VDIM


def _cparams(sem):
    return pltpu.CompilerParams(dimension_semantics=sem, vmem_limit_bytes=VMEM_LIMIT_BYTES)


def _const_spec(shape):
    zeros = (0,) * len(shape)
    return pl.BlockSpec(shape, lambda *_: zeros, pipeline_mode=pl.Buffered(1))


def _dot(a, b):
    return jnp.dot(a, b, preferred_element_type=F32)


def _dot_nt(a, b):
    return lax.dot_general(a, b, (((1,), (1,)), ((), ())), preferred_element_type=F32)


def _dot_tn(a, b):
    return lax.dot_general(a, b, (((0,), (0,)), ((), ())), preferred_element_type=F32)


def _sigmoid(x):
    return 1.0 / (1.0 + jnp.exp(-x))


def _silu(x):
    return x * _sigmoid(x)


def _rms(x, w):
    ms = jnp.mean(x * x, axis=-1, keepdims=True)
    return x * lax.rsqrt(ms + EPS) * w


def _split_hi_lo(x):
    hi = x.astype(BF16)
    lo = (x - hi.astype(F32)).astype(BF16)
    return hi, lo


def _mem_kv_kernel(mem_ref, nw_ref, wkv_ref, wq_ref, wo_ref, wqk_ref, wvo_ref):
    m = _rms(mem_ref[0], nw_ref[...]).astype(BF16)
    kv = _dot(m, wkv_ref[...])
    km = kv[:, :D_MODEL].astype(BF16)
    vm = kv[:, D_MODEL:].astype(BF16)
    mlen = km.shape[0]
    for hd in range(XA_HEADS):
        sl = slice(hd * XA_HEAD_DIM, (hd + 1) * XA_HEAD_DIM)
        ml = slice(hd * mlen, (hd + 1) * mlen)
        wqk_ref[0, :, ml] = (_dot_nt(wq_ref[:, sl], km[:, sl]) * (XA_HEAD_DIM ** -0.5)).astype(BF16)
        wvo_ref[0, ml, :] = _dot(vm[:, sl], wo_ref[sl, :]).astype(BF16)


def _mem_kv(mem, norm_w, wkv, wq, wo):
    bsz, mlen, d = mem.shape
    return pl.pallas_call(
        _mem_kv_kernel,
        grid=(bsz,),
        in_specs=[pl.BlockSpec((1, mlen, d), lambda b: (b, 0, 0)),
                  _const_spec((1, d)), _const_spec((d, 2 * d)),
                  _const_spec(wq.shape), _const_spec(wo.shape)],
        out_specs=[pl.BlockSpec((1, d, XA_HEADS * mlen), lambda b: (b, 0, 0)),
                   pl.BlockSpec((1, XA_HEADS * mlen, d), lambda b: (b, 0, 0))],
        out_shape=[jax.ShapeDtypeStruct((bsz, d, XA_HEADS * mlen), BF16),
                   jax.ShapeDtypeStruct((bsz, XA_HEADS * mlen, d), BF16)],
        compiler_params=_cparams(("arbitrary",)),
        name="mem_kv",
    )(mem, norm_w, wkv, wq, wo)


def _ssd_kernel(x_ref, nmw_ref, w_ref, convw_ref, convb_ref, dtb_ref, alog_ref,
                dskip_ref, nw_ref, expand_ref, shift_ref, o_ref,
                proj_ref, ext_ref, state_ref, y_ref, zs_ref):
    n_seq, q = x_ref.shape[0], x_ref.shape[1]
    gw = SSD_DIM // SSD_GROUPS
    heads_per_group = SSD_HEADS // SSD_GROUPS
    step = pl.program_id(0)
    seqs = range(n_seq)

    @pl.when(step == 0)
    def _():
        proj_ref[...] = jnp.zeros_like(proj_ref)

    @pl.when(step <= 1)
    def _():
        ext_ref[:, 0:CONV_TAIL, :] = jnp.zeros((n_seq, CONV_TAIL, SSD_CONV_CH), BF16)
        state_ref[...] = jnp.zeros_like(state_ref)

    @pl.when(step > 1)
    def _():
        ext_ref[:, 0:CONV_TAIL, :] = ext_ref[:, q:q + CONV_TAIL, :]

    row = lax.broadcasted_iota(jnp.int32, (q, q), 0)
    colm = lax.broadcasted_iota(jnp.int32, (q, q), 1)
    causal = row >= colm
    tri = causal.astype(BF16)
    lane = lax.broadcasted_iota(jnp.int32, (q, LANES), 1)
    left = lane < SSD_HEAD_DIM

    h_next = _rms(x_ref[...].reshape(n_seq * q, D_MODEL), nmw_ref[...]).astype(BF16)
    n_pieces = -(-SSD_PROJ // SSD_PROJ_PIECE)

    def project(j):
        cols = slice(j * SSD_PROJ_PIECE, min((j + 1) * SSD_PROJ_PIECE, SSD_PROJ))
        proj_ref[:, cols] = _dot(h_next, w_ref[:, cols])

    xs, xdt, xw, b_all, c_all, acum, acum_t, dfs_w, cd_w = ([None] * n_seq for _ in range(9))
    u_cur, dtr = [None] * n_seq, [None] * n_seq
    for i in seqs:
        rows = slice(i * q, (i + 1) * q)
        zs_ref[i] = _silu(proj_ref[rows, 0:SSD_DIM])
        u_cur[i] = proj_ref[rows, SSD_DIM:SSD_DIM + SSD_CONV_CH].astype(BF16)
        ext_ref[i, CONV_TAIL:CONV_TAIL + q, :] = u_cur[i]
        dtr[i] = proj_ref[rows, SSD_DIM + SSD_CONV_CH:] + dtb_ref[...]

    for i in seqs:
        shifted = _dot(shift_ref[...], ext_ref[i])
        project(2 * i)
        acc = convb_ref[...] + convw_ref[SSD_CONV - 1:SSD_CONV, :] * u_cur[i].astype(F32)
        for k in range(SSD_CONV - 1):
            acc = acc + convw_ref[k:k + 1, :] * shifted[k * q:(k + 1) * q]
        xbc = _silu(acc)
        xs[i] = xbc[:, :SSD_DIM]
        b_all[i] = xbc[:, SSD_DIM:SSD_DIM + SSD_GROUPS * SSD_STATE]
        c_all[i] = xbc[:, SSD_DIM + SSD_GROUPS * SSD_STATE:]

        dt = jnp.maximum(dtr[i], 0.0) + jnp.log(1.0 + jnp.exp(-jnp.abs(dtr[i])))
        da = dt * (-jnp.exp(alog_ref[...]) * LOG2E)
        da_hi, da_lo = _split_hi_lo(da)
        da_lo2 = (da - da_hi.astype(F32) - da_lo.astype(F32)).astype(BF16)
        acum[i] = _dot(tri, da_hi) + _dot(tri, da_lo) + _dot(tri, da_lo2)
        project(2 * i + 1)
        acum_t[i] = acum[i].T
        a_last = acum[i][q - 1:q, :]
        dec_from_start = jnp.exp2(acum[i])
        dec_to_end = jnp.exp2(a_last - acum[i])
        chunk_decay = jnp.broadcast_to(jnp.exp2(a_last), (BF16_SUBLANES, LANES))

        stack = jnp.concatenate([dt, dec_from_start, dec_to_end], axis=0).astype(BF16)
        wide = _dot(stack, expand_ref[...])
        cd_hi, cd_lo = _split_hi_lo(chunk_decay)
        cd_w[i] = (_dot(cd_hi, expand_ref[...]) + _dot(cd_lo, expand_ref[...]))[0:1]
        dfs_w[i] = wide[q:2 * q]
        xdt[i] = xs[i] * wide[0:q]
        xw[i] = (xdt[i] * wide[2 * q:3 * q]).astype(BF16)

    for g in range(SSD_GROUPS):
        gsl = slice(g * gw, (g + 1) * gw)
        for i in seqs:
            b_g = b_all[i][:, g * SSD_STATE:(g + 1) * SSD_STATE].astype(BF16)
            c_g = c_all[i][:, g * SSD_STATE:(g + 1) * SSD_STATE].astype(BF16)
            cb = _dot_nt(c_g, b_g)
            s_in = state_ref[i, g]
            y_ref[i, :, gsl] = _dot(c_g, s_in.astype(BF16)) * dfs_w[i][:, gsl]
            state_ref[i, g] = s_in * cd_w[i][:, gsl] + _dot_tn(b_g, xw[i][:, gsl])
            for pair in range(heads_per_group // 2):
                h0 = g * heads_per_group + 2 * pair
                ms = []
                for h in (h0, h0 + 1):
                    seg = acum[i][:, h:h + 1] - acum_t[i][h:h + 1, :]
                    l_dec = jnp.exp2(jnp.where(causal, seg, -jnp.inf))
                    ms.append((cb * l_dec).astype(BF16))
                lhs = jnp.concatenate(ms, axis=1)
                psl = slice(h0 * SSD_HEAD_DIM, (h0 + 2) * SSD_HEAD_DIM)
                xp = xdt[i][:, psl]
                rhs = jnp.concatenate([jnp.where(left, xp, 0.0), jnp.where(left, 0.0, xp)],
                                      axis=0).astype(BF16)
                y_ref[i, :, psl] += _dot(lhs, rhs)

    for j in range(2 * n_seq, n_pieces):
        project(j)
    for i in seqs:
        yz = (y_ref[i] + dskip_ref[...] * xs[i]) * zs_ref[i]
        for g in range(SSD_GROUPS):
            gsl = slice(g * gw, (g + 1) * gw)
            o_ref[i, :, gsl] = _rms(yz[:, gsl], nw_ref[:, gsl]).astype(o_ref.dtype)


def _conv_shift_matrix(q):
    s = np.zeros(((SSD_CONV - 1) * q, CONV_TAIL + q), np.float32)
    for k in range(SSD_CONV - 1):
        for t in range(q):
            s[k * q + t, CONV_TAIL + t - (SSD_CONV - 1) + k] = 1.0
    return jnp.asarray(s, BF16)


def _lookahead_specs(n_seq, n_chunks, rows, width_in, width_out):
    x_spec = pl.BlockSpec((n_seq, rows, width_in),
                          lambda s: (0, jnp.minimum(s, n_chunks - 1), 0))
    o_spec = pl.BlockSpec((n_seq, rows, width_out), lambda s: (0, jnp.maximum(s - 1, 0), 0))
    return x_spec, o_spec


def _ssd(x, norm_mix_w, w_ssd, conv_w, conv_b, dt_bias, a_log, d_skip_w, norm_w, expand):
    n_seq, seqlen, d = x.shape
    q = SSD_CHUNK
    n_chunks = seqlen // q
    shift = _conv_shift_matrix(q)
    x_spec, o_spec = _lookahead_specs(n_seq, n_chunks, q, d, SSD_DIM)
    consts = (norm_mix_w, w_ssd, conv_w, conv_b, dt_bias, a_log, d_skip_w, norm_w, expand, shift)
    return pl.pallas_call(
        _ssd_kernel,
        grid=(n_chunks + 1,),
        in_specs=[x_spec] + [_const_spec(c.shape) for c in consts],
        out_specs=o_spec,
        out_shape=jax.ShapeDtypeStruct((n_seq, seqlen, SSD_DIM), BF16),
        scratch_shapes=[pltpu.VMEM((n_seq * q, SSD_PROJ), F32),
                        pltpu.VMEM((n_seq, CONV_TAIL + q, SSD_CONV_CH), BF16),
                        pltpu.VMEM((n_seq, SSD_GROUPS, SSD_STATE, SSD_DIM // SSD_GROUPS), F32),
                        pltpu.VMEM((n_seq, q, SSD_DIM), F32),
                        pltpu.VMEM((n_seq, q, SSD_DIM), F32)],
        compiler_params=_cparams(("arbitrary",)),
        name="ssd",
    )(x, *consts)


HG_PROJ_PIECES = 8
HG_VPU_MIN_HALF = 8


def _hgrn2_levels(c):
    levels = []
    s = c // 2
    while s >= 1:
        levels.append(s)
        s //= 2
    return levels


def _hgrn2_tables(c):
    idx = np.arange(c)
    blocks = [np.tril(np.ones((c, c), np.float32))]
    masks = []
    for s in _hgrn2_levels(c):
        mid = (idx // (2 * s)) * (2 * s) + s
        upper = idx >= mid
        if 2 < s < HG_VPU_MIN_HALF:
            a = np.zeros((c, c), np.float32)
            for i in range(c):
                if upper[i]:
                    a[i, mid[i]:i + 1] = 1.0
                else:
                    a[i, i + 1:mid[i]] = 1.0
            blocks.append(a)
        same = (idx[:, None] // (2 * s)) == (idx[None, :] // (2 * s))
        masks.append((same & upper[:, None] & (~upper)[None, :]).astype(np.float32))
    masks.append(np.eye(c, dtype=np.float32))
    sums = np.concatenate(blocks, axis=0)
    sums2 = np.concatenate([sums, sums], axis=1)
    masks = np.stack(masks)
    masks2 = np.concatenate([masks, masks], axis=2)
    return jnp.asarray(sums2, BF16), jnp.asarray(masks2, F32)


def _hgrn2_kernel(x_ref, nmw_ref, w_ref, lbp_ref, nw_ref, sums_ref, mask_ref, o_ref,
                  proj_ref, state_ref, g2_ref, q_ref, k_ref, b_ref, ql_ref, klt_ref, qe_ref,
                  ke_ref, v_ref, gate_ref):
    n_seq, c = x_ref.shape[0], x_ref.shape[1]
    levels = _hgrn2_levels(c)
    n_levels = len(levels)
    step = pl.program_id(0)
    seqs = range(n_seq)

    @pl.when(step == 0)
    def _():
        proj_ref[...] = jnp.zeros_like(proj_ref)

    @pl.when(step <= 1)
    def _():
        state_ref[...] = jnp.zeros_like(state_ref)

    lbp = lbp_ref[...]
    e = jnp.exp(lbp - jnp.max(lbp, axis=0, keepdims=True))
    lb = e[0:1, :] / jnp.sum(e, axis=0, keepdims=True)

    for i in seqs:
        rows = slice(i * c, (i + 1) * c)
        q_ref[i] = _silu(proj_ref[rows, 0:HG_KDIM])
        fg = lb + (1.0 - lb) * _sigmoid(proj_ref[rows, HG_KDIM:2 * HG_KDIM])
        v_ref[i] = proj_ref[rows, 2 * HG_KDIM:2 * HG_KDIM + HG_VDIM].astype(BF16)
        gate_ref[i] = _silu(proj_ref[rows, 2 * HG_KDIM + HG_VDIM:])
        g_hi, g_lo = _split_hi_lo(jnp.log2(fg))
        g2_ref[i, 0:c, :] = g_hi
        g2_ref[i, c:2 * c, :] = g_lo
        k_ref[i] = 1.0 - fg
        ql_ref[i, n_levels - 1] = (q_ref[i] * fg).astype(BF16)
        r4 = lax.broadcasted_iota(jnp.int32, fg.shape, 0) % 4
        f_dn = pltpu.roll(fg, 1, axis=0)
        f_up = pltpu.roll(fg, c - 1, axis=0)
        ql_ref[i, n_levels - 2] = (q_ref[i] * jnp.where(r4 == 3, f_dn * fg, fg)).astype(BF16)
        klt_ref[i, n_levels - 2] = (k_ref[i] * jnp.where(r4 == 0, f_up, 1.0)).T

    h_next = _rms(x_ref[...].reshape(n_seq * c, D_MODEL), nmw_ref[...]).astype(BF16)
    piece = HG_PROJ // HG_PROJ_PIECES

    def project(j):
        cols = slice(j * piece, (j + 1) * piece)
        proj_ref[:, cols] = _dot(h_next, w_ref[:, cols])

    fine, state_decay = [None] * n_seq, [None] * n_seq

    def cumsums(i):
        fine[i] = _dot(sums_ref[...], g2_ref[i])
        b_ref[i] = fine[i][0:c]
        b_last = b_ref[i, c - 1:c, :]
        qe_ref[i] = (q_ref[i] * jnp.exp2(b_ref[i])).astype(BF16)
        ke_ref[i] = (k_ref[i] * jnp.exp2(b_last - b_ref[i])).astype(BF16)
        state_decay[i] = jnp.exp2(b_last)

    def level(i, lvl):
        if lvl >= n_levels - 2 and lvl < n_levels:
            return
        if lvl == n_levels:
            ql, kl = q_ref[i], k_ref[i]
        elif levels[lvl] >= HG_VPU_MIN_HALF:
            s = levels[lvl]
            q_rows, k_rows = [], []
            for r0 in range(0, c, 2 * s):
                lo = slice(r0, r0 + s)
                up = slice(r0 + s, r0 + 2 * s)
                ref = b_ref[i, r0 + s - 1:r0 + s, :]
                zeros = jnp.zeros((s, HG_KDIM), F32)
                q_rows += [zeros, q_ref[i, up, :] * jnp.exp2(b_ref[i, up, :] - ref)]
                k_rows += [k_ref[i, lo, :] * jnp.exp2(ref - b_ref[i, lo, :]), zeros]
            ql = jnp.concatenate(q_rows, axis=0)
            kl = jnp.concatenate(k_rows, axis=0)
        else:
            n_fine = 1 + sum(1 for s in levels[:lvl] if 2 < s < HG_VPU_MIN_HALF)
            ex = jnp.exp2(fine[i][n_fine * c:(n_fine + 1) * c])
            ql = q_ref[i] * ex
            kl = k_ref[i] * ex
        ql_ref[i, lvl] = ql.astype(BF16)
        klt_ref[i, lvl] = kl.T

    zero = jnp.zeros((HG_K, c), BF16)

    def head_pair(i, pair):
        psl = slice(2 * pair * HG_K, (2 * pair + 2) * HG_K)
        def block_diag(lvl):
            kt = klt_ref[i, lvl, psl, :].astype(BF16)
            return jnp.concatenate(
                [jnp.concatenate([kt[:HG_K], zero], axis=1),
                 jnp.concatenate([zero, kt[HG_K:]], axis=1)], axis=0)

        att = _dot(ql_ref[i, 0, :, psl], block_diag(0))
        for lvl in range(1, n_levels - 1):
            att = att + _dot(ql_ref[i, lvl, :, psl], block_diag(lvl)) * mask_ref[lvl]
        kbd = block_diag(n_levels)
        for lvl in (n_levels - 1, n_levels):
            att = att + _dot(ql_ref[i, lvl, :, psl], kbd) * mask_ref[lvl]
        for sub in range(2):
            h = 2 * pair + sub
            sl = slice(h * HG_K, (h + 1) * HG_K)
            vh = v_ref[i, :, sl]
            s_t = state_ref[i, h]
            o = _dot_nt(qe_ref[i, :, sl], s_t.astype(BF16))
            o = o + _dot(att[:, sub * c:(sub + 1) * c].astype(BF16), vh)
            state_ref[i, h] = s_t * state_decay[i][:, sl] + _dot_tn(vh, ke_ref[i, :, sl])
            o_ref[i, :, sl] = (_rms(o, nw_ref[...]) * gate_ref[i, :, sl]).astype(o_ref.dtype)

    project(0)
    project(1)
    for i in seqs:
        cumsums(i)
    n_pairs = HG_HEADS // 2
    per_pair = -(-(n_levels + 1) // n_pairs)
    for lvl in range(n_levels + 1):
        level(0, lvl)
        if 2 + lvl < HG_PROJ_PIECES:
            project(2 + lvl)
    for i in seqs:
        for pair in range(n_pairs):
            head_pair(i, pair)
            if i + 1 < n_seq:
                for lvl in range(pair * per_pair, min((pair + 1) * per_pair, n_levels + 1)):
                    level(i + 1, lvl)


def _hgrn2(x, norm_mix_w, w_hg, lb_params, norm_w):
    n_seq, seqlen, d = x.shape
    c = HG_CHUNK
    n_chunks = seqlen // c
    n_levels = len(_hgrn2_levels(c))
    sums, masks = _hgrn2_tables(c)
    consts = (norm_mix_w, w_hg, lb_params, norm_w, sums, masks)
    x_spec, o_spec = _lookahead_specs(n_seq, n_chunks, c, d, HG_VDIM)
    wide = lambda dt, *lead: pltpu.VMEM((n_seq, *lead, c, HG_KDIM), dt)
    return pl.pallas_call(
        _hgrn2_kernel,
        grid=(n_chunks + 1,),
        in_specs=[x_spec] + [_const_spec(a.shape) for a in consts],
        out_specs=o_spec,
        out_shape=jax.ShapeDtypeStruct((n_seq, seqlen, HG_VDIM), BF16),
        scratch_shapes=[pltpu.VMEM((n_seq * c, HG_PROJ), F32),
                        pltpu.VMEM((n_seq, HG_HEADS, HG_V, HG_K), F32),
                        pltpu.VMEM((n_seq, 2 * c, HG_KDIM), BF16),
                        wide(F32), wide(F32), wide(F32),
                        wide(BF16, n_levels + 1),
                        pltpu.VMEM((n_seq, n_levels + 1, HG_KDIM, c), F32),
                        wide(BF16), wide(BF16),
                        wide(BF16), wide(F32)],
        compiler_params=_cparams(("arbitrary",)),
        name="hgrn2",
    )(x, *consts)


def _attn_kernel(x_ref, ya_ref, ob_ref, wout_ref, nw_ref, wqk_ref, wvo_ref, o_ref, p_ref):
    x1 = x_ref[...] + _dot(ya_ref[...], wout_ref[0:SSD_DIM, :]) \
        + _dot(ob_ref[...], wout_ref[SSD_DIM:, :])
    h = _rms(x1, nw_ref[...]).astype(BF16)
    scores = _dot(h, wqk_ref[0])
    mlen = wqk_ref.shape[2] // XA_HEADS
    for hd in range(XA_HEADS):
        ml = slice(hd * mlen, (hd + 1) * mlen)
        sc = scores[:, ml]
        p = jnp.exp(sc - jnp.max(sc, axis=-1, keepdims=True))
        p_ref[:, ml] = (p / jnp.sum(p, axis=-1, keepdims=True)).astype(BF16)
    o_ref[...] = x1 + _dot(p_ref[...], wvo_ref[0])


def _attn(x2d, ya, ob, w_out, norm_w, wqk, wvo, seqlen):
    t, d = x2d.shape
    tm = min(ATTN_ROWS, seqlen)
    tpb = seqlen // tm
    n_mem = wqk.shape[2]
    row = lambda n: pl.BlockSpec((tm, n), lambda i: (i, 0))
    return pl.pallas_call(
        _attn_kernel,
        grid=(t // tm,),
        in_specs=[row(d), row(SSD_DIM), row(HG_VDIM), _const_spec(w_out.shape),
                  _const_spec(norm_w.shape),
                  pl.BlockSpec((1, d, n_mem), lambda i: (i // tpb, 0, 0)),
                  pl.BlockSpec((1, n_mem, d), lambda i: (i // tpb, 0, 0))],
        out_specs=row(d),
        out_shape=jax.ShapeDtypeStruct((t, d), F32),
        scratch_shapes=[pltpu.VMEM((tm, n_mem), BF16)],
        compiler_params=_cparams(("arbitrary",)),
        name="attn",
    )(x2d, ya, ob, w_out, norm_w, wqk, wvo)


def _ffn_kernel(x_ref, nw_ref, wg_ref, wu_ref, wd_ref, nf_ref, o_ref, acc_ref):
    x = x_ref[...]
    h = _rms(x, nw_ref[...]).astype(BF16)
    acc_ref[...] = x
    n_cols = wg_ref.shape[1]
    for c0 in range(0, n_cols, FFN_COLS):
        g = _dot(h, wg_ref[:, c0:c0 + FFN_COLS])
        u = _dot(h, wu_ref[:, c0:c0 + FFN_COLS])
        a = (_silu(g) * u).astype(BF16)
        acc_ref[...] += _dot(a, wd_ref[c0:c0 + FFN_COLS, :])
    o_ref[...] = _rms(acc_ref[...], nf_ref[...])


def _ffn(x2d, norm_w, wg, wu, wd, norm_final):
    t, d = x2d.shape
    tm = min(FFN_ROWS, t)
    row = pl.BlockSpec((tm, d), lambda i: (i, 0))
    return pl.pallas_call(
        _ffn_kernel,
        grid=(t // tm,),
        in_specs=[row, _const_spec(norm_w.shape), _const_spec(wg.shape), _const_spec(wu.shape),
                  _const_spec(wd.shape), _const_spec(norm_final.shape)],
        out_specs=row,
        out_shape=jax.ShapeDtypeStruct((t, d), F32),
        scratch_shapes=[pltpu.VMEM((tm, d), F32)],
        compiler_params=_cparams(("arbitrary",)),
        name="ffn",
    )(x2d, norm_w, wg, wu, wd, norm_final)


def _pad_lanes(v):
    return jnp.pad(v.astype(F32), (0, LANES - v.shape[0])).reshape(1, LANES)


def kernel(x, mem, norm_mix_w, w_in, conv_w, conv_b, dt_bias, a_log, d_skip, ssd_norm_w,
           hg_lower_bounds, hg_norm_w, w_out, norm_xa_w, norm_mem_w, xa_wq, xa_wkv, xa_wo,
           norm_ffn_w, ffn_w_gate, ffn_w_up, ffn_w_down, norm_final_w):
    bsz, seqlen, d = x.shape
    assert d == D_MODEL and norm_mix_w.shape[0] == 1, "single-layer block of width 1024"
    assert hg_lower_bounds.shape[0] == 2
    assert seqlen % SSD_CHUNK == 0 and seqlen % HG_CHUNK == 0
    t = bsz * seqlen
    x2d = x.reshape(t, d)
    row = lambda v: v.reshape(1, -1).astype(F32)

    s3 = SSD_DIM + SSD_CONV_CH + SSD_HEADS
    w = w_in[0]
    w_ssd = jnp.pad(w[:, :s3], ((0, 0), (0, LANES - SSD_HEADS))).astype(BF16)
    w_hg = w[:, s3:].astype(BF16)

    expand = (np.arange(LANES)[:, None] == (np.arange(SSD_DIM)[None, :] // SSD_HEAD_DIM))
    expand = jnp.asarray(expand, BF16)
    d_skip_w = jnp.repeat(d_skip[0].astype(F32), SSD_HEAD_DIM).reshape(1, SSD_DIM)
    y_a = _ssd(x, row(norm_mix_w[0]), w_ssd, conv_w[0].astype(F32), row(conv_b[0]),
               _pad_lanes(dt_bias[0]), _pad_lanes(a_log[0]), d_skip_w, row(ssd_norm_w[0]),
               expand).reshape(t, SSD_DIM)
    o_b = _hgrn2(x, row(norm_mix_w[0]), w_hg, hg_lower_bounds.astype(F32),
                 row(hg_norm_w[0])).reshape(t, HG_VDIM)

    wqk, wvo = _mem_kv(mem, row(norm_mem_w[0]), xa_wkv[0].astype(BF16), xa_wq[0].astype(BF16),
                       xa_wo[0].astype(BF16))
    x2 = _attn(x2d, y_a, o_b, w_out[0].astype(BF16), row(norm_xa_w[0]), wqk, wvo, seqlen)
    out = _ffn(x2, row(norm_ffn_w[0]), ffn_w_gate[0].astype(BF16), ffn_w_up[0].astype(BF16),
               ffn_w_down[0].astype(BF16), row(norm_final_w))
    return out.reshape(bsz, seqlen, d)
```
